```python
import jax, jax.numpy as jnp
from jax import lax
import numpy as np

D_MODEL = 1024
BATCH = 2
SEQ = 8192
DEPTH = 1

MLA_HEADS = 8
MLA_Q_LORA = 256
MLA_KV_LORA = 128
MLA_NOPE_DIM = 64
MLA_ROPE_DIM = 32
MLA_V_DIM = 64
FOX_HEADS = 8
FOX_HEAD_DIM = 64
D_FF = 4 * D_MODEL
Q_BLOCK = 128
ROPE_THETA = 10000.0
NORM_EPS = 1e-6
MAX_POS_OFFSET = 1024
FORGET_BIAS_LO = 1.0
FORGET_BIAS_HI = 6.0

IN_SPLITS = (MLA_Q_LORA, MLA_KV_LORA, MLA_ROPE_DIM,
             FOX_HEADS * FOX_HEAD_DIM, FOX_HEADS * FOX_HEAD_DIM, FOX_HEADS * FOX_HEAD_DIM,
             FOX_HEADS, D_MODEL, D_MODEL)
D_IN = sum(IN_SPLITS)
FORGET_COL_START = sum(IN_SPLITS[:6])

kernel_name = "hybrid_mla_fox_gated_block"


def rms_norm(x, g):
    xf = x.astype(jnp.float32)
    y = xf * lax.rsqrt(jnp.mean(xf * xf, axis=-1, keepdims=True) + NORM_EPS)
    return (y * g.astype(jnp.float32)).astype(x.dtype)


def rope(x, positions):
    half = x.shape[-1] // 2
    inv_freq = ROPE_THETA ** (-jnp.arange(half, dtype=jnp.float32) / half)
    ang = positions.astype(jnp.float32)[:, None, :, None] * inv_freq
    cos, sin = jnp.cos(ang), jnp.sin(ang)
    xf = x.astype(jnp.float32)
    x1, x2 = xf[..., :half], xf[..., half:]
    out = jnp.concatenate([x1 * cos - x2 * sin, x2 * cos + x1 * sin], axis=-1)
    return out.astype(x.dtype)


def causal_block_attention(q, k, v, scale, log_f_cum=None):
    B, H, S, _ = q.shape
    dv = v.shape[-1]
    n_blocks = S // Q_BLOCK
    kf = k.astype(jnp.float32)
    vf = v.astype(jnp.float32)
    k_pos = jnp.arange(S)

    def one_block(i):
        start = i * Q_BLOCK
        qb = lax.dynamic_slice_in_dim(q, start, Q_BLOCK, axis=2).astype(jnp.float32)
        s = jnp.einsum('bhqd,bhkd->bhqk', qb, kf) * scale
        if log_f_cum is not None:
            fq = lax.dynamic_slice_in_dim(log_f_cum, start, Q_BLOCK, axis=2)
            s = s + fq[..., :, None] - log_f_cum[..., None, :]
        q_pos = start + jnp.arange(Q_BLOCK)
        mask = k_pos[None, :] <= q_pos[:, None]
        s = jnp.where(mask, s, -jnp.inf)
        p = jax.nn.softmax(s, axis=-1)
        return jnp.einsum('bhqk,bhkd->bhqd', p, vf)

    out = lax.map(one_block, jnp.arange(n_blocks))
    out = jnp.moveaxis(out, 0, 2).reshape(B, H, S, dv)
    return out.astype(v.dtype)


def split_heads(t, n_heads):
    B, S, _ = t.shape
    return t.reshape(B, S, n_heads, -1).transpose(0, 2, 1, 3)


def merge_heads(t):
    B, H, S, d = t.shape
    return t.transpose(0, 2, 1, 3).reshape(B, S, H * d)


def setup_inputs(seed: int = 0) -> dict:
    key = jax.random.key(seed)
    ks = jax.random.split(key, 20)
    L = DEPTH

    def w(k, shape, fan_in):
        return jax.random.normal(k, shape, jnp.float32) * fan_in ** -0.5

    def gain(k, shape):
        return 1.0 + 0.05 * jax.random.normal(k, shape, jnp.float32)

    x = jax.random.normal(ks[0], (BATCH, SEQ, D_MODEL), jnp.float32)
    offset = jax.random.randint(ks[1], (BATCH, 1), 0, MAX_POS_OFFSET, dtype=jnp.int32)
    positions = (offset + jnp.arange(SEQ, dtype=jnp.int32)[None, :]).astype(jnp.int32)

    b_in = 0.02 * jax.random.normal(ks[2], (L, D_IN), jnp.float32)
    forget_bias = jnp.linspace(FORGET_BIAS_LO, FORGET_BIAS_HI, FOX_HEADS, dtype=jnp.float32)
    b_in = b_in.at[:, FORGET_COL_START:FORGET_COL_START + FOX_HEADS].add(forget_bias)

    return {
        "x": x,
        "positions": positions,
        "ln_pre_mix": gain(ks[3], (L, D_MODEL)),
        "ln_post_mix": gain(ks[4], (L, D_MODEL)),
        "ln_pre_mlp": gain(ks[5], (L, D_MODEL)),
        "ln_post_mlp": gain(ks[6], (L, D_MODEL)),
        "w_in": w(ks[7], (L, D_MODEL, D_IN), D_MODEL),
        "b_in": b_in,
        "q_a_norm": gain(ks[8], (L, MLA_Q_LORA)),
        "w_uq": w(ks[9], (L, MLA_Q_LORA, MLA_HEADS * (MLA_NOPE_DIM + MLA_ROPE_DIM)), MLA_Q_LORA),
        "kv_a_norm": gain(ks[10], (L, MLA_KV_LORA)),
        "w_uk": w(ks[11], (L, MLA_KV_LORA, MLA_HEADS * MLA_NOPE_DIM), MLA_KV_LORA),
        "w_uv": w(ks[12], (L, MLA_KV_LORA, MLA_HEADS * MLA_V_DIM), MLA_KV_LORA),
        "w_o_mla": w(ks[13], (L, MLA_HEADS * MLA_V_DIM, D_MODEL), MLA_HEADS * MLA_V_DIM),
        "w_o_fox": w(ks[14], (L, FOX_HEADS * FOX_HEAD_DIM, D_MODEL), FOX_HEADS * FOX_HEAD_DIM),
        "w_out": w(ks[15], (L, D_MODEL, D_MODEL), D_MODEL),
        "w_ff1": w(ks[16], (L, D_MODEL, D_FF), D_MODEL),
        "w_ff2": w(ks[17], (L, D_FF, D_MODEL), D_FF),
    }


def reference(x, positions, ln_pre_mix, ln_post_mix, ln_pre_mlp, ln_post_mlp,
              w_in, b_in, q_a_norm, w_uq, kv_a_norm, w_uk, w_uv,
              w_o_mla, w_o_fox, w_out, w_ff1, w_ff2):
    split_idx = [int(v) for v in np.cumsum(IN_SPLITS)[:-1]]
    for l in range(DEPTH):
        h = rms_norm(x, ln_pre_mix[l])
        z = jnp.einsum('bsd,de->bse', h, w_in[l]) + b_in[l]
        (c_q, c_kv, k_r, fq, fk, fv, f_logit, ga_logit, gb_logit) = jnp.split(z, split_idx, axis=-1)

        c_q = rms_norm(c_q, q_a_norm[l])
        q_full = split_heads(jnp.einsum('bsr,re->bse', c_q, w_uq[l]), MLA_HEADS)
        q_nope, q_rot = q_full[..., :MLA_NOPE_DIM], q_full[..., MLA_NOPE_DIM:]
        q_rot = rope(q_rot, positions)
        c_kv = rms_norm(c_kv, kv_a_norm[l])
        k_nope = split_heads(jnp.einsum('bsr,re->bse', c_kv, w_uk[l]), MLA_HEADS)
        v_mla = split_heads(jnp.einsum('bsr,re->bse', c_kv, w_uv[l]), MLA_HEADS)
        k_rot = rope(k_r[:, None, :, :], positions)
        B, H, S, _ = k_nope.shape
        q_mla = jnp.concatenate([q_nope, q_rot], axis=-1)
        k_mla = jnp.concatenate([k_nope, jnp.broadcast_to(k_rot, (B, H, S, MLA_ROPE_DIM))], axis=-1)
        o_mla = causal_block_attention(q_mla, k_mla, v_mla, (MLA_NOPE_DIM + MLA_ROPE_DIM) ** -0.5)
        y_mla = jnp.einsum('bse,ed->bsd', merge_heads(o_mla), w_o_mla[l])

        q_fox = split_heads(fq, FOX_HEADS)
        k_fox = split_heads(fk, FOX_HEADS)
        v_fox = split_heads(fv, FOX_HEADS)
        log_f = jax.nn.log_sigmoid(f_logit.astype(jnp.float32))
        log_f_cum = jnp.cumsum(log_f, axis=1).transpose(0, 2, 1)
        o_fox = causal_block_attention(q_fox, k_fox, v_fox, FOX_HEAD_DIM ** -0.5, log_f_cum)
        y_fox = jnp.einsum('bse,ed->bsd', merge_heads(o_fox), w_o_fox[l])

        merged = jax.nn.sigmoid(ga_logit) * y_mla + jax.nn.sigmoid(gb_logit) * y_fox
        mix_out = jnp.einsum('bsd,de->bse', merged, w_out[l])
        x = x + rms_norm(mix_out, ln_post_mix[l])

        h2 = rms_norm(x, ln_pre_mlp[l])
        u = jnp.einsum('bsd,df->bsf', h2, w_ff1[l])
        m = jnp.einsum('bsf,fd->bsd', jnp.square(jax.nn.relu(u)), w_ff2[l])
        x = x + rms_norm(m, ln_post_mlp[l])
    return x
```

```python
import functools

import numpy as np
import jax
import jax.numpy as jnp
from jax import lax
from jax.experimental import pallas as pl
from jax.experimental.pallas import tpu as pltpu

F32 = jnp.float32
BF16 = jnp.bfloat16

D_MODEL = 1024
MLA_HEADS = 8
MLA_Q_LORA = 256
MLA_KV_LORA = 128
MLA_NOPE_DIM = 64
MLA_ROPE_DIM = 32
MLA_V_DIM = 64
FOX_HEADS = 8
FOX_HEAD_DIM = 64
D_FF = 4 * D_MODEL
ROPE_THETA = 10000.0
NORM_EPS = 1e-6
IN_SPLITS = (MLA_Q_LORA, MLA_KV_LORA, MLA_ROPE_DIM,
             FOX_HEADS * FOX_HEAD_DIM, FOX_HEADS * FOX_HEAD_DIM, FOX_HEADS * FOX_HEAD_DIM,
             FOX_HEADS, D_MODEL, D_MODEL)

LANES = 128
HEAD_PAD = 128
HALF_ROPE = MLA_ROPE_DIM // 2
N_PAIRS = MLA_HEADS // 2
VMEM_LIMIT = 52 * 1024 * 1024

_W = (MLA_Q_LORA, MLA_KV_LORA, LANES, 512, 512, 512, LANES, D_MODEL, D_MODEL)
_OFF = tuple(int(v) for v in np.cumsum((0,) + _W))
(C_CQ, C_CKV, C_KR, C_FQ, C_FK, C_FV, C_F, C_GA, C_GB) = tuple(
    (_OFF[i], _OFF[i + 1]) for i in range(9))
D_IN_PAD = _OFF[-1]

PRE_TM = 512
ATT_TQ = 512
ATT_TK = 512
POST_TM = 512
FF_CHUNK = 512


def _rms(x, g):
    return x * lax.rsqrt(jnp.mean(x * x, axis=-1, keepdims=True) + NORM_EPS) * g


def _const_spec(shape):
    zeros = (0,) * len(shape)
    return pl.BlockSpec(shape, lambda *_: zeros, pipeline_mode=pl.Buffered(1))


def _pre_kernel(x_ref, pos_ref, g_ref, w_ref, b_ref, qn_ref, wuq_ref, kvn_ref, wuk_ref,
                wuv_ref, freq_ref, sgn_ref,
                qm_ref, km_ref, vm_ref, qf_ref, kf_ref, vf_ref, fc_ref, ga_ref, gb_ref,
                carry_ref):
    tm = x_ref.shape[1]

    @pl.when(pl.program_id(1) == 0)
    def _():
        carry_ref[...] = jnp.zeros_like(carry_ref)

    h = _rms(x_ref[0], g_ref[...]).astype(BF16)

    def proj(c):
        return (jnp.dot(h, w_ref[:, c[0]:c[1]], preferred_element_type=F32)
                + b_ref[:, c[0]:c[1]])

    ang = pos_ref[0].astype(F32) * freq_ref[...]
    cos = jnp.cos(ang)
    sin = jnp.sin(ang)
    s_lo = sin * sgn_ref[0:1, :]
    s_hi = sin * sgn_ref[1:2, :]

    def rope(blk):
        return (blk * cos + pltpu.roll(blk, LANES - HALF_ROPE, 1) * s_lo
                + pltpu.roll(blk, HALF_ROPE, 1) * s_hi)

    c_q = _rms(proj(C_CQ), qn_ref[...]).astype(BF16)
    q_full = jnp.dot(c_q, wuq_ref[...], preferred_element_type=F32)
    q_scale = (MLA_NOPE_DIM + MLA_ROPE_DIM) ** -0.5
    c_kv = _rms(proj(C_CKV), kvn_ref[...]).astype(BF16)
    k_nope = jnp.dot(c_kv, wuk_ref[...], preferred_element_type=F32)
    k_rot = rope(proj(C_KR))
    for hd in range(MLA_HEADS):
        sl = slice(hd * HEAD_PAD, (hd + 1) * HEAD_PAD)
        qm_ref[0, :, sl] = (rope(q_full[:, sl]) * q_scale).astype(BF16)
        km_ref[0, :, sl] = (k_nope[:, sl] + k_rot).astype(BF16)
    vm_ref[0] = jnp.dot(c_kv, wuv_ref[...], preferred_element_type=F32).astype(BF16)

    qf_ref[0] = (proj(C_FQ) * (FOX_HEAD_DIM ** -0.5)).astype(BF16)
    kf_ref[0] = proj(C_FK).astype(BF16)
    vf_ref[0] = proj(C_FV).astype(BF16)

    f_logit = proj(C_F)
    log_f = jnp.minimum(f_logit, 0.0) - jnp.log1p(jnp.exp(-jnp.abs(f_logit)))
    row = lax.broadcasted_iota(jnp.int32, (tm, tm), 0)
    col = lax.broadcasted_iota(jnp.int32, (tm, tm), 1)
    tri = (row >= col).astype(BF16)
    p0 = log_f.astype(BF16)
    r1 = log_f - p0.astype(F32)
    p1 = r1.astype(BF16)
    p2 = (r1 - p1.astype(F32)).astype(BF16)
    cum = (jnp.dot(tri, p0, preferred_element_type=F32)
           + jnp.dot(tri, p1, preferred_element_type=F32)
           + jnp.dot(tri, p2, preferred_element_type=F32)) + carry_ref[...]
    fc_ref[0] = cum
    carry_ref[...] = cum[tm - 1:tm, :]

    ga_ref[0] = jax.nn.sigmoid(proj(C_GA)).astype(BF16)
    gb_ref[0] = jax.nn.sigmoid(proj(C_GB)).astype(BF16)


def _pre_call(x, pos3, g, w_r, b_r, qn, wuq_r, kvn, wuk_r, wuv, freq, sgn):
    B, S, D = x.shape
    tm = PRE_TM
    tok = lambda n: pl.BlockSpec((1, tm, n), lambda b, j: (b, j, 0))
    out_shape = [
        jax.ShapeDtypeStruct((B, S, MLA_HEADS * HEAD_PAD), BF16),
        jax.ShapeDtypeStruct((B, S, MLA_HEADS * HEAD_PAD), BF16),
        jax.ShapeDtypeStruct((B, S, MLA_HEADS * MLA_V_DIM), BF16),
        jax.ShapeDtypeStruct((B, S, 512), BF16),
        jax.ShapeDtypeStruct((B, S, 512), BF16),
        jax.ShapeDtypeStruct((B, S, 512), BF16),
        jax.ShapeDtypeStruct((B, S, LANES), F32),
        jax.ShapeDtypeStruct((B, S, D), BF16),
        jax.ShapeDtypeStruct((B, S, D), BF16),
    ]
    in_specs = [tok(D), tok(1)] + [_const_spec(a.shape) for a in
                                   (g, w_r, b_r, qn, wuq_r, kvn, wuk_r, wuv, freq, sgn)]
    out_specs = [tok(s.shape[-1]) for s in out_shape]
    return pl.pallas_call(
        _pre_kernel,
        grid=(B, S // tm),
        in_specs=in_specs,
        out_specs=out_specs,
        out_shape=out_shape,
        scratch_shapes=[pltpu.VMEM((1, LANES), F32)],
        compiler_params=pltpu.CompilerParams(
            dimension_semantics=("parallel", "arbitrary"), vmem_limit_bytes=VMEM_LIMIT),
        name="pre",
    )(x, pos3, g, w_r, b_r, qn, wuq_r, kvn, wuk_r, wuv, freq, sgn)


def _attn_kernel(*refs, tq, tk, fox):
    if fox:
        q_ref, k_ref, v_ref, fcol_ref, frow_ref, o_ref, qs_ref, m_ref, l_ref, acc_ref = refs
    else:
        q_ref, k_ref, v_ref, o_ref, m_ref, l_ref, acc_ref = refs
    i = pl.program_id(2)
    lane = lax.broadcasted_iota(jnp.int32, (tq, LANES), 1)

    if fox:
        qb = q_ref[0]
        qs_ref[0] = jnp.where(lane < FOX_HEAD_DIM, qb, jnp.zeros_like(qb))
        qs_ref[1] = jnp.where(lane >= FOX_HEAD_DIM, qb, jnp.zeros_like(qb))

    m_ref[...] = jnp.full(m_ref.shape, -jnp.inf, F32)
    l_ref[...] = jnp.zeros_like(l_ref)
    acc_ref[...] = jnp.zeros_like(acc_ref)

    def tile(j, masked):
        off = pl.multiple_of(j * tk, tk)
        v = v_ref[0, pl.ds(off, tk), :]
        for hd in range(2):
            if fox:
                q = qs_ref[hd]
                k = k_ref[0, pl.ds(off, tk), :]
            else:
                q = q_ref[0, :, hd * HEAD_PAD:(hd + 1) * HEAD_PAD]
                k = k_ref[0, pl.ds(off, tk), hd * HEAD_PAD:(hd + 1) * HEAD_PAD]
            s = lax.dot_general(q, k, (((1,), (1,)), ((), ())), preferred_element_type=F32)
            if fox:
                s = s + fcol_ref[0, hd] - frow_ref[0, hd, :, pl.ds(off, tk)]
            if masked:
                r = lax.broadcasted_iota(jnp.int32, (tq, tk), 0)
                c = lax.broadcasted_iota(jnp.int32, (tq, tk), 1)
                s = jnp.where(r >= c, s, -jnp.inf)
            m_prev = m_ref[hd]
            m_new = jnp.maximum(m_prev, jnp.max(s, axis=1, keepdims=True))
            p = jnp.exp(s - m_new)
            alpha = jnp.exp(m_prev - m_new)
            l_ref[hd] = alpha * l_ref[hd] + jnp.sum(p, axis=1, keepdims=True)
            acc_ref[hd] = alpha * acc_ref[hd] + jnp.dot(
                p.astype(BF16), v, preferred_element_type=F32)
            m_ref[hd] = m_new

    def body(j, c):
        tile(j, False)
        return c

    lax.fori_loop(0, i, body, 0)
    tile(i, True)

    o = jnp.where(lane < MLA_V_DIM, acc_ref[0] / l_ref[0], acc_ref[1] / l_ref[1])
    o_ref[0] = o.astype(o_ref.dtype)


def _attn_call(q, k, v, fcol=None, frow=None):
    B, S, _ = v.shape
    fox = fcol is not None
    tq, tk = ATT_TQ, ATT_TK
    assert tq == tk
    dqk = q.shape[-1] // N_PAIRS
    in_specs = [
        pl.BlockSpec((1, tq, dqk), lambda b, p, i: (b, i, p)),
        pl.BlockSpec((1, S, dqk), lambda b, p, i: (b, 0, p)),
        pl.BlockSpec((1, S, LANES), lambda b, p, i: (b, 0, p)),
    ]
    args = [q, k, v]
    scratch = []
    if fox:
        in_specs += [
            pl.BlockSpec((1, 2, tq, 1), lambda b, p, i: (b, p, i, 0)),
            pl.BlockSpec((1, 2, 1, S), lambda b, p, i: (b, p, 0, 0)),
        ]
        args += [fcol, frow]
        scratch.append(pltpu.VMEM((2, tq, LANES), BF16))
    scratch += [pltpu.VMEM((2, tq, 1), F32), pltpu.VMEM((2, tq, 1), F32),
                pltpu.VMEM((2, tq, LANES), F32)]
    return pl.pallas_call(
        functools.partial(_attn_kernel, tq=tq, tk=tk, fox=fox),
        grid=(B, N_PAIRS, S // tq),
        in_specs=in_specs,
        out_specs=pl.BlockSpec((1, tq, LANES), lambda b, p, i: (b, i, p)),
        out_shape=jax.ShapeDtypeStruct((B, S, N_PAIRS * LANES), BF16),
        scratch_shapes=scratch,
        compiler_params=pltpu.CompilerParams(
            dimension_semantics=("parallel", "parallel", "arbitrary"),
            vmem_limit_bytes=VMEM_LIMIT),
        name="attn_fox" if fox else "attn_mla",
    )(*args)


def _post_kernel(om_ref, of_ref, ga_ref, gb_ref, x_ref, wom_ref, wof_ref, wout_ref,
                 g1_ref, g2_ref, w1_ref, w2_ref, g3_ref, out_ref):
    y_mla = jnp.dot(om_ref[0], wom_ref[...], preferred_element_type=F32)
    y_fox = jnp.dot(of_ref[0], wof_ref[...], preferred_element_type=F32)
    merged = ga_ref[0].astype(F32) * y_mla + gb_ref[0].astype(F32) * y_fox
    mix = jnp.dot(merged.astype(BF16), wout_ref[...], preferred_element_type=F32)
    x1 = x_ref[0] + _rms(mix, g1_ref[...])
    h2 = _rms(x1, g2_ref[...]).astype(BF16)
    m = jnp.zeros_like(x1)
    for c in range(D_FF // FF_CHUNK):
        sl = slice(c * FF_CHUNK, (c + 1) * FF_CHUNK)
        u = jnp.dot(h2, w1_ref[:, sl], preferred_element_type=F32)
        a = jnp.square(jnp.maximum(u, 0.0)).astype(BF16)
        m = m + jnp.dot(a, w2_ref[sl, :], preferred_element_type=F32)
    out_ref[0] = x1 + _rms(m, g3_ref[...])


def _post_call(o_mla, o_fox, ga, gb, x, wom, wof, wout, g1, g2, w1, w2, g3):
    B, S, D = x.shape
    tm = POST_TM
    tok = lambda n: pl.BlockSpec((1, tm, n), lambda b, j: (b, j, 0))
    in_specs = [tok(512), tok(512), tok(D), tok(D), tok(D)] + [
        _const_spec(a.shape) for a in (wom, wof, wout, g1, g2, w1, w2, g3)]
    return pl.pallas_call(
        _post_kernel,
        grid=(B, S // tm),
        in_specs=in_specs,
        out_specs=tok(D),
        out_shape=jax.ShapeDtypeStruct((B, S, D), F32),
        compiler_params=pltpu.CompilerParams(
            dimension_semantics=("parallel", "parallel"), vmem_limit_bytes=VMEM_LIMIT),
        name="post",
    )(o_mla, o_fox, ga, gb, x, wom, wof, wout, g1, g2, w1, w2, g3)


def _arrange_in_proj(w, b):
    offs = np.cumsum((0,) + IN_SPLITS)
    cols = [w[:, offs[i]:offs[i + 1]] for i in range(9)]
    bias = [b[offs[i]:offs[i + 1]] for i in range(9)]
    kr_pad = (MLA_NOPE_DIM, LANES - MLA_NOPE_DIM - MLA_ROPE_DIM)
    f_pad = (0, LANES - FOX_HEADS)
    cols[2] = jnp.pad(cols[2], ((0, 0), kr_pad))
    bias[2] = jnp.pad(bias[2], kr_pad)
    cols[6] = jnp.pad(cols[6], ((0, 0), f_pad))
    bias[6] = jnp.pad(bias[6], f_pad)
    return (jnp.concatenate(cols, axis=1).astype(BF16),
            jnp.concatenate(bias)[None, :].astype(F32))


def _pad_heads(w, width):
    r = w.shape[0]
    w = w.reshape(r, MLA_HEADS, width)
    w = jnp.pad(w, ((0, 0), (0, 0), (0, HEAD_PAD - width)))
    return w.reshape(r, MLA_HEADS * HEAD_PAD).astype(BF16)


def _rope_tables():
    inv = ROPE_THETA ** (-np.arange(HALF_ROPE, dtype=np.float32) / HALF_ROPE)
    freq = np.zeros((1, LANES), np.float32)
    freq[0, MLA_NOPE_DIM:MLA_NOPE_DIM + HALF_ROPE] = inv
    freq[0, MLA_NOPE_DIM + HALF_ROPE:MLA_NOPE_DIM + MLA_ROPE_DIM] = inv
    sgn = np.zeros((2, LANES), np.float32)
    sgn[0, MLA_NOPE_DIM:MLA_NOPE_DIM + HALF_ROPE] = -1.0
    sgn[1, MLA_NOPE_DIM + HALF_ROPE:MLA_NOPE_DIM + MLA_ROPE_DIM] = 1.0
    return jnp.asarray(freq), jnp.asarray(sgn)


def _layer(x, pos3, ln_pre_mix, ln_post_mix, ln_pre_mlp, ln_post_mlp, w_in, b_in, q_a_norm,
           w_uq, kv_a_norm, w_uk, w_uv, w_o_mla, w_o_fox, w_out, w_ff1, w_ff2):
    B, S, _ = x.shape
    row = lambda v: v[None, :].astype(F32)
    w_r, b_r = _arrange_in_proj(w_in, b_in)
    freq, sgn = _rope_tables()
    (q_mla, k_mla, v_mla, q_fox, k_fox, v_fox, fcum, ga, gb) = _pre_call(
        x, pos3, row(ln_pre_mix), w_r, b_r, row(q_a_norm),
        _pad_heads(w_uq, MLA_NOPE_DIM + MLA_ROPE_DIM), row(kv_a_norm),
        _pad_heads(w_uk, MLA_NOPE_DIM), w_uv.astype(BF16), freq, sgn)

    f_heads = jnp.transpose(fcum[:, :, :FOX_HEADS], (0, 2, 1))
    fcol = f_heads[:, :, :, None]
    frow = f_heads[:, :, None, :]

    o_mla = _attn_call(q_mla, k_mla, v_mla)
    o_fox = _attn_call(q_fox, k_fox, v_fox, fcol, frow)

    return _post_call(o_mla, o_fox, ga, gb, x, w_o_mla.astype(BF16), w_o_fox.astype(BF16),
                      w_out.astype(BF16), row(ln_post_mix), row(ln_pre_mlp),
                      w_ff1.astype(BF16), w_ff2.astype(BF16), row(ln_post_mlp))


def kernel(x, positions, ln_pre_mix, ln_post_mix, ln_pre_mlp, ln_post_mlp, w_in, b_in,
           q_a_norm, w_uq, kv_a_norm, w_uk, w_uv, w_o_mla, w_o_fox, w_out, w_ff1, w_ff2):
    pos3 = positions[:, :, None]
    for l in range(w_in.shape[0]):
        x = _layer(x, pos3, ln_pre_mix[l], ln_post_mix[l], ln_pre_mlp[l], ln_post_mlp[l],
                   w_in[l], b_in[l], q_a_norm[l], w_uq[l], kv_a_norm[l], w_uk[l], w_uv[l],
                   w_o_mla[l], w_o_fox[l], w_out[l], w_ff1[l], w_ff2[l])
    return x
```

```python
import functools

import numpy as np
import jax
import jax.numpy as jnp
from jax import lax
from jax.experimental import pallas as pl
from jax.experimental.pallas import tpu as pltpu

F32 = jnp.float32
BF16 = jnp.bfloat16

D_MODEL = 1024
MLA_HEADS = 8
MLA_Q_LORA = 256
MLA_KV_LORA = 128
MLA_NOPE_DIM = 64
MLA_ROPE_DIM = 32
MLA_V_DIM = 64
FOX_HEADS = 8
FOX_HEAD_DIM = 64
D_FF = 4 * D_MODEL
ROPE_THETA = 10000.0
NORM_EPS = 1e-6
IN_SPLITS = (MLA_Q_LORA, MLA_KV_LORA, MLA_ROPE_DIM,
             FOX_HEADS * FOX_HEAD_DIM, FOX_HEADS * FOX_HEAD_DIM, FOX_HEADS * FOX_HEAD_DIM,
             FOX_HEADS, D_MODEL, D_MODEL)

LANES = 128
HEAD_PAD = 128
HEADS = 8
HEAD_V = 64
HALF_ROPE = MLA_ROPE_DIM // 2
N_PAIRS = HEADS // 2
AUG = 64
LOG2E = float(np.log2(np.e))
VMEM_LIMIT = 52 * 1024 * 1024

BLK = 512
POST_TM = 512
FF_CHUNK = 512

_W = (MLA_Q_LORA, MLA_KV_LORA, LANES, HEADS * HEAD_PAD, HEADS * HEAD_PAD, LANES, D_MODEL, D_MODEL)
_OFF = tuple(int(v) for v in np.cumsum((0,) + _W))
(C_CQ, C_CKV, C_KR, C_FQ, C_FK, C_F, C_GA, C_GB) = tuple(
    (_OFF[i], _OFF[i + 1]) for i in range(8))


def _rms(x, g):
    return x * lax.rsqrt(jnp.mean(x * x, axis=-1, keepdims=True) + NORM_EPS) * g


def _const_spec(shape):
    zeros = (0,) * len(shape)
    return pl.BlockSpec(shape, lambda *_: zeros, pipeline_mode=pl.Buffered(1))


def _dot_nt(a, b):
    return lax.dot_general(a, b, (((1,), (1,)), ((), ())), preferred_element_type=F32)


def _pre_kernel(x_ref, pos_ref, g_ref, w_ref, b_ref, qn_ref, wuq_ref, kvn_ref, wuk_ref,
                wuvt_ref, wfvt_ref, bfvt_ref, freq_ref, sgn_ref, selq_ref, selk_ref, one_ref,
                vfill_ref,
                qm_ref, km_ref, vmt_ref, qf_ref, kf_ref, vft_ref, goff_ref, ga_ref, gb_ref,
                carry_ref):
    tm = x_ref.shape[1]

    @pl.when(pl.program_id(1) == 0)
    def _():
        carry_ref[...] = jnp.zeros_like(carry_ref)

    h = _rms(x_ref[0], g_ref[...]).astype(BF16)

    def proj(c):
        return (jnp.dot(h, w_ref[:, c[0]:c[1]], preferred_element_type=F32)
                + b_ref[:, c[0]:c[1]])

    ang = pos_ref[0].astype(F32) * freq_ref[...]
    cos = jnp.cos(ang)
    sin = jnp.sin(ang)
    s_lo = sin * sgn_ref[0:1, :]
    s_hi = sin * sgn_ref[1:2, :]

    def rope(blk):
        return (blk * cos + pltpu.roll(blk, LANES - HALF_ROPE, 1) * s_lo
                + pltpu.roll(blk, HALF_ROPE, 1) * s_hi)

    c_q = _rms(proj(C_CQ), qn_ref[...]).astype(BF16)
    q_full = jnp.dot(c_q, wuq_ref[...], preferred_element_type=F32)
    q_scale = (MLA_NOPE_DIM + MLA_ROPE_DIM) ** -0.5 * LOG2E
    c_kv = _rms(proj(C_CKV), kvn_ref[...]).astype(BF16)
    k_nope = jnp.dot(c_kv, wuk_ref[...], preferred_element_type=F32)
    k_rot = rope(proj(C_KR))
    for hd in range(HEADS):
        sl = slice(hd * HEAD_PAD, (hd + 1) * HEAD_PAD)
        qm_ref[0, :, sl] = (rope(q_full[:, sl]) * q_scale).astype(BF16)
        km_ref[0, :, sl] = (k_nope[:, sl] + k_rot).astype(BF16)
    vmt = _dot_nt(wuvt_ref[...], c_kv).astype(BF16)

    f_logit = proj(C_F)
    log_f = jnp.minimum(f_logit, 0.0) - jnp.log1p(jnp.exp(-jnp.abs(f_logit)))
    def split3(v):
        p0 = v.astype(BF16)
        r1 = v - p0.astype(F32)
        p1 = r1.astype(BF16)
        p2 = (r1 - p1.astype(F32)).astype(BF16)
        return p0, p1, p2

    row = lax.broadcasted_iota(jnp.int32, (tm, tm), 0)
    col = lax.broadcasted_iota(jnp.int32, (tm, tm), 1)
    tri = (row >= col).astype(BF16)
    local = sum(jnp.dot(tri, p, preferred_element_type=F32) for p in split3(log_f))
    goff_ref[0, 0] = carry_ref[...] * LOG2E
    carry_ref[...] = carry_ref[...] + local[tm - 1:tm, :]

    pieces = split3(local * LOG2E)
    q_aug = one_ref[0:1, :]
    k_aug = one_ref[1:2, :]
    for r, p in enumerate(pieces):
        q_aug = q_aug + jnp.dot(p, selq_ref[r], preferred_element_type=F32)
        k_aug = k_aug + jnp.dot(p, selk_ref[r], preferred_element_type=F32)
    qf_ref[0] = (proj(C_FQ) * (FOX_HEAD_DIM ** -0.5 * LOG2E) + q_aug).astype(BF16)
    kf_ref[0] = (proj(C_FK) + k_aug).astype(BF16)
    vft = (_dot_nt(wfvt_ref[...], h) + bfvt_ref[...]).astype(BF16)

    vfill = vfill_ref[...]
    for hd in range(HEADS):
        rows = slice(hd * HEAD_V, (hd + 1) * HEAD_V)
        vmt_ref[0, hd, 0, 0:HEAD_V, :] = vmt[rows, :]
        vmt_ref[0, hd, 0, HEAD_V:, :] = vfill
        vft_ref[0, hd, 0, 0:HEAD_V, :] = vft[rows, :]
        vft_ref[0, hd, 0, HEAD_V:, :] = vfill

    ga_ref[0] = jax.nn.sigmoid(proj(C_GA)).astype(BF16)
    gb_ref[0] = jax.nn.sigmoid(proj(C_GB)).astype(BF16)


def _pre_call(x, pos3, consts):
    B, S, D = x.shape
    tm = BLK
    nblk = S // tm
    tok = lambda n: pl.BlockSpec((1, tm, n), lambda b, j: (b, j, 0))
    vt_shape = jax.ShapeDtypeStruct((B, HEADS, nblk, HEAD_PAD, tm), BF16)
    vt_spec = pl.BlockSpec((1, HEADS, 1, HEAD_PAD, tm), lambda b, j: (b, 0, j, 0, 0))
    out_shape = [
        jax.ShapeDtypeStruct((B, S, HEADS * HEAD_PAD), BF16),
        jax.ShapeDtypeStruct((B, S, HEADS * HEAD_PAD), BF16),
        vt_shape,
        jax.ShapeDtypeStruct((B, S, HEADS * HEAD_PAD), BF16),
        jax.ShapeDtypeStruct((B, S, HEADS * HEAD_PAD), BF16),
        vt_shape,
        jax.ShapeDtypeStruct((B, nblk, 1, LANES), F32),
        jax.ShapeDtypeStruct((B, S, D), BF16),
        jax.ShapeDtypeStruct((B, S, D), BF16),
    ]
    out_specs = [tok(HEADS * HEAD_PAD), tok(HEADS * HEAD_PAD), vt_spec,
                 tok(HEADS * HEAD_PAD), tok(HEADS * HEAD_PAD), vt_spec,
                 pl.BlockSpec((1, 1, 1, LANES), lambda b, j: (b, j, 0, 0)),
                 tok(D), tok(D)]
    in_specs = [tok(D), tok(1)] + [_const_spec(a.shape) for a in consts]
    return pl.pallas_call(
        _pre_kernel,
        grid=(B, nblk),
        in_specs=in_specs,
        out_specs=out_specs,
        out_shape=out_shape,
        scratch_shapes=[pltpu.VMEM((1, LANES), F32)],
        compiler_params=pltpu.CompilerParams(
            dimension_semantics=("parallel", "arbitrary"), vmem_limit_bytes=VMEM_LIMIT),
        name="pre",
    )(x, pos3, *consts)


def _attn_kernel(*refs, fox):
    if fox:
        goff_ref, q_ref, k_ref, vt_ref, o_ref, m_ref, acc_ref = refs
    else:
        q_ref, k_ref, vt_ref, o_ref, m_ref, acc_ref = refs
    t = BLK
    b = pl.program_id(0)
    pair = pl.program_id(1)
    i = pl.program_id(2)

    m_ref[...] = jnp.full(m_ref.shape, -jnp.inf, F32)
    acc_ref[...] = jnp.zeros_like(acc_ref)

    def tile(j, masked):
        off = pl.multiple_of(j * t, t)
        for hd in range(2):
            lanes = slice(hd * HEAD_PAD, (hd + 1) * HEAD_PAD)
            s = _dot_nt(k_ref[0, pl.ds(off, t), lanes], q_ref[0, :, lanes])
            if masked:
                key = lax.broadcasted_iota(jnp.int32, (t, t), 0)
                qry = lax.broadcasted_iota(jnp.int32, (t, t), 1)
                s = jnp.where(key <= qry, s, -jnp.inf)
            mx = jnp.max(s, axis=0, keepdims=True)
            m_prev = m_ref[hd]
            if fox:
                g_row = (b * N_PAIRS + pair) * 2 + hd
                delta = goff_ref[g_row, i] - goff_ref[g_row, j]
                m_new = jnp.maximum(m_prev, mx + delta)
                shift = m_new - delta
            else:
                m_new = jnp.maximum(m_prev, mx)
                shift = m_new
            p = jnp.exp2(s - shift).astype(BF16)
            alpha = jnp.exp2(m_prev - m_new)
            acc_ref[hd] = alpha * acc_ref[hd] + jnp.dot(
                vt_ref[0, hd, j], p, preferred_element_type=F32)
            m_ref[hd] = m_new

    def body(j, c):
        tile(j, False)
        return c

    lax.fori_loop(0, i, body, 0)
    tile(i, True)

    outs = []
    for hd in range(2):
        acc = acc_ref[hd]
        outs.append(acc[0:HEAD_V, :] / acc[HEAD_V:HEAD_V + 1, :])
    o_ref[0] = jnp.transpose(jnp.concatenate(outs, axis=0)).astype(o_ref.dtype)


def _attn_call(q, k, vt, goff=None):
    B, S, _ = q.shape
    fox = goff is not None
    t = BLK
    nblk = S // t
    in_specs = [
        pl.BlockSpec((1, t, 2 * HEAD_PAD), lambda b, p, i: (b, i, p)),
        pl.BlockSpec((1, S, 2 * HEAD_PAD), lambda b, p, i: (b, 0, p)),
        pl.BlockSpec((1, 2, nblk, HEAD_PAD, t), lambda b, p, i: (b, p, 0, 0, 0)),
    ]
    args = [q, k, vt]
    if fox:
        in_specs = [pl.BlockSpec(memory_space=pltpu.SMEM)] + in_specs
        args = [goff] + args
    return pl.pallas_call(
        functools.partial(_attn_kernel, fox=fox),
        grid=(B, N_PAIRS, nblk),
        in_specs=in_specs,
        out_specs=pl.BlockSpec((1, t, 2 * HEAD_V), lambda b, p, i: (b, i, p)),
        out_shape=jax.ShapeDtypeStruct((B, S, HEADS * HEAD_V), BF16),
        scratch_shapes=[pltpu.VMEM((2, 1, t), F32), pltpu.VMEM((2, HEAD_PAD, t), F32)],
        compiler_params=pltpu.CompilerParams(
            dimension_semantics=("parallel", "parallel", "arbitrary"),
            vmem_limit_bytes=VMEM_LIMIT),
        name="attn_fox" if fox else "attn_mla",
    )(*args)


def _post_kernel(om_ref, of_ref, ga_ref, gb_ref, x_ref, wom_ref, wof_ref, wout_ref,
                 g1_ref, g2_ref, w1_ref, w2_ref, g3_ref, out_ref):
    y_mla = jnp.dot(om_ref[0], wom_ref[...], preferred_element_type=F32)
    y_fox = jnp.dot(of_ref[0], wof_ref[...], preferred_element_type=F32)
    merged = ga_ref[0].astype(F32) * y_mla + gb_ref[0].astype(F32) * y_fox
    mix = jnp.dot(merged.astype(BF16), wout_ref[...], preferred_element_type=F32)
    x1 = x_ref[0] + _rms(mix, g1_ref[...])
    h2 = _rms(x1, g2_ref[...]).astype(BF16)
    m = jnp.zeros_like(x1)
    for c in range(D_FF // FF_CHUNK):
        sl = slice(c * FF_CHUNK, (c + 1) * FF_CHUNK)
        u = jnp.dot(h2, w1_ref[:, sl], preferred_element_type=F32)
        a = jnp.square(jnp.maximum(u, 0.0)).astype(BF16)
        m = m + jnp.dot(a, w2_ref[sl, :], preferred_element_type=F32)
    out_ref[0] = x1 + _rms(m, g3_ref[...])


def _post_call(o_mla, o_fox, ga, gb, x, wom, wof, wout, g1, g2, w1, w2, g3):
    B, S, D = x.shape
    tm = POST_TM
    tok = lambda n: pl.BlockSpec((1, tm, n), lambda b, j: (b, j, 0))
    in_specs = [tok(512), tok(512), tok(D), tok(D), tok(D)] + [
        _const_spec(a.shape) for a in (wom, wof, wout, g1, g2, w1, w2, g3)]
    return pl.pallas_call(
        _post_kernel,
        grid=(B, S // tm),
        in_specs=in_specs,
        out_specs=tok(D),
        out_shape=jax.ShapeDtypeStruct((B, S, D), F32),
        compiler_params=pltpu.CompilerParams(
            dimension_semantics=("parallel", "parallel"), vmem_limit_bytes=VMEM_LIMIT),
        name="post",
    )(o_mla, o_fox, ga, gb, x, wom, wof, wout, g1, g2, w1, w2, g3)


def _pad_heads(w, width):
    r = w.shape[0]
    w = w.reshape(r, HEADS, width)
    w = jnp.pad(w, ((0, 0), (0, 0), (0, HEAD_PAD - width)))
    return w.reshape(r, HEADS * HEAD_PAD)


def _arrange_in_proj(w, b):
    offs = np.cumsum((0,) + IN_SPLITS)
    cq, ckv, kr, fq, fk, fv, f, ga, gb = [w[:, offs[i]:offs[i + 1]] for i in range(9)]
    bcq, bckv, bkr, bfq, bfk, bfv, bf, bga, bgb = [b[None, offs[i]:offs[i + 1]] for i in range(9)]
    kr_pad = ((0, 0), (MLA_NOPE_DIM, LANES - MLA_NOPE_DIM - MLA_ROPE_DIM))
    f_pad = ((0, 0), (0, LANES - FOX_HEADS))
    w_r = jnp.concatenate([cq, ckv, jnp.pad(kr, kr_pad), _pad_heads(fq, FOX_HEAD_DIM),
                           _pad_heads(fk, FOX_HEAD_DIM), jnp.pad(f, f_pad), ga, gb], axis=1)
    b_r = jnp.concatenate([bcq, bckv, jnp.pad(bkr, kr_pad), _pad_heads(bfq, FOX_HEAD_DIM),
                           _pad_heads(bfk, FOX_HEAD_DIM), jnp.pad(bf, f_pad), bga, bgb], axis=1)
    return w_r.astype(BF16), b_r.astype(F32), fv.T.astype(BF16), bfv.T.astype(F32)


def _static_tables():
    inv = ROPE_THETA ** (-np.arange(HALF_ROPE, dtype=np.float32) / HALF_ROPE)
    freq = np.zeros((1, LANES), np.float32)
    freq[0, MLA_NOPE_DIM:MLA_NOPE_DIM + HALF_ROPE] = inv
    freq[0, MLA_NOPE_DIM + HALF_ROPE:MLA_NOPE_DIM + MLA_ROPE_DIM] = inv
    sgn = np.zeros((2, LANES), np.float32)
    sgn[0, MLA_NOPE_DIM:MLA_NOPE_DIM + HALF_ROPE] = -1.0
    sgn[1, MLA_NOPE_DIM + HALF_ROPE:MLA_NOPE_DIM + MLA_ROPE_DIM] = 1.0
    selq = np.zeros((3, LANES, HEADS * HEAD_PAD), np.float32)
    selk = np.zeros((3, LANES, HEADS * HEAD_PAD), np.float32)
    ones = np.zeros((2, HEADS * HEAD_PAD), np.float32)
    for hd in range(HEADS):
        base = hd * HEAD_PAD + AUG
        for r in range(3):
            selq[r, hd, base + r] = 1.0
            selk[r, hd, base + 3 + r] = -1.0
        ones[0, base + 3:base + 6] = 1.0
        ones[1, base:base + 3] = 1.0
    vfill = np.zeros((HEAD_PAD - HEAD_V, BLK), np.float32)
    vfill[0, :] = 1.0
    return (jnp.asarray(freq), jnp.asarray(sgn), jnp.asarray(selq, BF16), jnp.asarray(selk, BF16),
            jnp.asarray(ones), jnp.asarray(vfill, BF16))


def _layer(x, pos3, ln_pre_mix, ln_post_mix, ln_pre_mlp, ln_post_mlp, w_in, b_in, q_a_norm,
           w_uq, kv_a_norm, w_uk, w_uv, w_o_mla, w_o_fox, w_out, w_ff1, w_ff2):
    B, S, _ = x.shape
    row = lambda v: v[None, :].astype(F32)
    w_r, b_r, wfvt, bfvt = _arrange_in_proj(w_in, b_in)
    freq, sgn, selq, selk, ones, vfill = _static_tables()
    consts = (row(ln_pre_mix), w_r, b_r, row(q_a_norm),
              _pad_heads(w_uq, MLA_NOPE_DIM + MLA_ROPE_DIM).astype(BF16), row(kv_a_norm),
              _pad_heads(w_uk, MLA_NOPE_DIM).astype(BF16), w_uv.T.astype(BF16), wfvt, bfvt,
              freq, sgn, selq, selk, ones, vfill)
    (q_mla, k_mla, vt_mla, q_fox, k_fox, vt_fox, goff, ga, gb) = _pre_call(x, pos3, consts)

    goff = jnp.transpose(goff[:, :, 0, :FOX_HEADS], (0, 2, 1)).reshape(B * FOX_HEADS, S // BLK)

    o_mla = _attn_call(q_mla, k_mla, vt_mla)
    o_fox = _attn_call(q_fox, k_fox, vt_fox, goff)

    return _post_call(o_mla, o_fox, ga, gb, x, w_o_mla.astype(BF16), w_o_fox.astype(BF16),
                      w_out.astype(BF16), row(ln_post_mix), row(ln_pre_mlp),
                      w_ff1.astype(BF16), w_ff2.astype(BF16), row(ln_post_mlp))


def kernel(x, positions, ln_pre_mix, ln_post_mix, ln_pre_mlp, ln_post_mlp, w_in, b_in,
           q_a_norm, w_uq, kv_a_norm, w_uk, w_uv, w_o_mla, w_o_fox, w_out, w_ff1, w_ff2):
    pos3 = positions[:, :, None]
    for l in range(w_in.shape[0]):
        x = _layer(x, pos3, ln_pre_mix[l], ln_post_mix[l], ln_pre_mlp[l], ln_post_mlp[l],
                   w_in[l], b_in[l], q_a_norm[l], w_uq[l], kv_a_norm[l], w_uk[l], w_uv[l],
                   w_o_mla[l], w_o_fox[l], w_out[l], w_ff1[l], w_ff2[l])
    return x
```

```python
import functools

import numpy as np
import jax
import jax.numpy as jnp
from jax import lax
from jax.experimental import pallas as pl
from jax.experimental.pallas import tpu as pltpu

F32 = jnp.float32
BF16 = jnp.bfloat16

D_MODEL = 1024
MLA_HEADS = 8
MLA_Q_LORA = 256
MLA_KV_LORA = 128
MLA_NOPE_DIM = 64
MLA_ROPE_DIM = 32
MLA_V_DIM = 64
FOX_HEADS = 8
FOX_HEAD_DIM = 64
D_FF = 4 * D_MODEL
ROPE_THETA = 10000.0
NORM_EPS = 1e-6
IN_SPLITS = (MLA_Q_LORA, MLA_KV_LORA, MLA_ROPE_DIM,
             FOX_HEADS * FOX_HEAD_DIM, FOX_HEADS * FOX_HEAD_DIM, FOX_HEADS * FOX_HEAD_DIM,
             FOX_HEADS, D_MODEL, D_MODEL)

LANES = 128
HEAD_PAD = 128
HEADS = 8
HEAD_V = 64
HALF_ROPE = MLA_ROPE_DIM // 2
N_PAIRS = HEADS // 2
AUG = 64
LOG2E = float(np.log2(np.e))
VMEM_LIMIT = 52 * 1024 * 1024

BLK = 512
POST_TM = 512
FF_CHUNK = 512

_W = (MLA_Q_LORA, MLA_KV_LORA, LANES, HEADS * HEAD_PAD, HEADS * HEAD_PAD, LANES, D_MODEL, D_MODEL)
_OFF = tuple(int(v) for v in np.cumsum((0,) + _W))
(C_CQ, C_CKV, C_KR, C_FQ, C_FK, C_F, C_GA, C_GB) = tuple(
    (_OFF[i], _OFF[i + 1]) for i in range(8))


def _rms(x, g):
    return x * lax.rsqrt(jnp.mean(x * x, axis=-1, keepdims=True) + NORM_EPS) * g


def _const_spec(shape):
    zeros = (0,) * len(shape)
    return pl.BlockSpec(shape, lambda *_: zeros, pipeline_mode=pl.Buffered(1))


def _dot_nt(a, b):
    return lax.dot_general(a, b, (((1,), (1,)), ((), ())), preferred_element_type=F32)


def _pre_kernel(x_ref, pos_ref, g_ref, w_ref, b_ref, qn_ref, wuq_ref, kvn_ref, wuk_ref,
                wuvt_ref, wfvt_ref, bfvt_ref, freq_ref, sgn_ref, selq_ref, selk_ref, one_ref,
                vfill_ref,
                qm_ref, km_ref, vmt_ref, qf_ref, kf_ref, vft_ref, goff_ref, ga_ref, gb_ref,
                carry_ref):
    tm = x_ref.shape[1]

    @pl.when(pl.program_id(1) == 0)
    def _():
        carry_ref[...] = jnp.zeros_like(carry_ref)

    h = _rms(x_ref[0], g_ref[...]).astype(BF16)

    def proj(c):
        return (jnp.dot(h, w_ref[:, c[0]:c[1]], preferred_element_type=F32)
                + b_ref[:, c[0]:c[1]])

    ang = pos_ref[0].astype(F32) * freq_ref[...]
    cos = jnp.cos(ang)
    sin = jnp.sin(ang)
    s_lo = sin * sgn_ref[0:1, :]
    s_hi = sin * sgn_ref[1:2, :]

    def rope(blk):
        return (blk * cos + pltpu.roll(blk, LANES - HALF_ROPE, 1) * s_lo
                + pltpu.roll(blk, HALF_ROPE, 1) * s_hi)

    c_q = _rms(proj(C_CQ), qn_ref[...]).astype(BF16)
    q_full = jnp.dot(c_q, wuq_ref[...], preferred_element_type=F32)
    q_scale = (MLA_NOPE_DIM + MLA_ROPE_DIM) ** -0.5 * LOG2E
    c_kv = _rms(proj(C_CKV), kvn_ref[...]).astype(BF16)
    k_nope = jnp.dot(c_kv, wuk_ref[...], preferred_element_type=F32)
    k_rot = rope(proj(C_KR))
    for hd in range(HEADS):
        sl = slice(hd * HEAD_PAD, (hd + 1) * HEAD_PAD)
        qm_ref[0, :, sl] = (rope(q_full[:, sl]) * q_scale).astype(BF16)
        km_ref[0, :, sl] = (k_nope[:, sl] + k_rot).astype(BF16)
    vmt = _dot_nt(wuvt_ref[...], c_kv).astype(BF16)

    f_logit = proj(C_F)
    log_f = jnp.minimum(f_logit, 0.0) - jnp.log1p(jnp.exp(-jnp.abs(f_logit)))
    def split3(v):
        p0 = v.astype(BF16)
        r1 = v - p0.astype(F32)
        p1 = r1.astype(BF16)
        p2 = (r1 - p1.astype(F32)).astype(BF16)
        return p0, p1, p2

    row = lax.broadcasted_iota(jnp.int32, (tm, tm), 0)
    col = lax.broadcasted_iota(jnp.int32, (tm, tm), 1)
    tri = (row >= col).astype(BF16)
    local = sum(jnp.dot(tri, p, preferred_element_type=F32) for p in split3(log_f))
    goff_ref[0, 0] = carry_ref[...] * LOG2E
    carry_ref[...] = carry_ref[...] + local[tm - 1:tm, :]

    pieces = split3(local * LOG2E)
    q_aug = one_ref[0:1, :]
    k_aug = one_ref[1:2, :]
    for r, p in enumerate(pieces):
        q_aug = q_aug + jnp.dot(p, selq_ref[r], preferred_element_type=F32)
        k_aug = k_aug + jnp.dot(p, selk_ref[r], preferred_element_type=F32)
    qf_ref[0] = (proj(C_FQ) * (FOX_HEAD_DIM ** -0.5 * LOG2E) + q_aug).astype(BF16)
    kf_ref[0] = (proj(C_FK) + k_aug).astype(BF16)
    vft = (_dot_nt(wfvt_ref[...], h) + bfvt_ref[...]).astype(BF16)

    vfill = vfill_ref[...]
    for hd in range(HEADS):
        rows = slice(hd * HEAD_V, (hd + 1) * HEAD_V)
        vmt_ref[0, hd, 0, 0:HEAD_V, :] = vmt[rows, :]
        vmt_ref[0, hd, 0, HEAD_V:, :] = vfill
        vft_ref[0, hd, 0, 0:HEAD_V, :] = vft[rows, :]
        vft_ref[0, hd, 0, HEAD_V:, :] = vfill

    ga_ref[0] = jax.nn.sigmoid(proj(C_GA)).astype(BF16)
    gb_ref[0] = jax.nn.sigmoid(proj(C_GB)).astype(BF16)


def _pre_call(x, pos3, consts):
    B, S, D = x.shape
    tm = BLK
    nblk = S // tm
    tok = lambda n: pl.BlockSpec((1, tm, n), lambda b, j: (b, j, 0))
    vt_shape = jax.ShapeDtypeStruct((B, HEADS, nblk, HEAD_PAD, tm), BF16)
    vt_spec = pl.BlockSpec((1, HEADS, 1, HEAD_PAD, tm), lambda b, j: (b, 0, j, 0, 0))
    out_shape = [
        jax.ShapeDtypeStruct((B, S, HEADS * HEAD_PAD), BF16),
        jax.ShapeDtypeStruct((B, S, HEADS * HEAD_PAD), BF16),
        vt_shape,
        jax.ShapeDtypeStruct((B, S, HEADS * HEAD_PAD), BF16),
        jax.ShapeDtypeStruct((B, S, HEADS * HEAD_PAD), BF16),
        vt_shape,
        jax.ShapeDtypeStruct((B, nblk, 1, LANES), F32),
        jax.ShapeDtypeStruct((B, S, D), BF16),
        jax.ShapeDtypeStruct((B, S, D), BF16),
    ]
    out_specs = [tok(HEADS * HEAD_PAD), tok(HEADS * HEAD_PAD), vt_spec,
                 tok(HEADS * HEAD_PAD), tok(HEADS * HEAD_PAD), vt_spec,
                 pl.BlockSpec((1, 1, 1, LANES), lambda b, j: (b, j, 0, 0)),
                 tok(D), tok(D)]
    in_specs = [tok(D), tok(1)] + [_const_spec(a.shape) for a in consts]
    return pl.pallas_call(
        _pre_kernel,
        grid=(B, nblk),
        in_specs=in_specs,
        out_specs=out_specs,
        out_shape=out_shape,
        scratch_shapes=[pltpu.VMEM((1, LANES), F32)],
        compiler_params=pltpu.CompilerParams(
            dimension_semantics=("parallel", "arbitrary"), vmem_limit_bytes=VMEM_LIMIT),
        name="pre",
    )(x, pos3, *consts)


def _attn_kernel(*refs, fox, nblk):
    if fox:
        goff_ref, q_ref, k_ref, vt_ref, o_ref, s_ref, p_ref, mx_ref, m_ref, a_ref, acc_ref = refs
    else:
        q_ref, k_ref, vt_ref, o_ref, s_ref, p_ref, mx_ref, m_ref, a_ref, acc_ref = refs
    t = BLK
    g_base = (pl.program_id(0) * N_PAIRS + pl.program_id(1)) * 2

    def stage_q(i, j, diag):
        qoff = pl.multiple_of(i * t, t)
        koff = pl.multiple_of(j * t, t)
        for hd in range(2):
            lanes = slice(hd * HEAD_PAD, (hd + 1) * HEAD_PAD)
            s = _dot_nt(k_ref[0, pl.ds(koff, t), lanes], q_ref[0, pl.ds(qoff, t), lanes])
            if diag:
                key = lax.broadcasted_iota(jnp.int32, (t, t), 0)
                qry = lax.broadcasted_iota(jnp.int32, (t, t), 1)
                s = jnp.where(key <= qry, s, -jnp.inf)
            s_ref[hd] = s
            mx_ref[hd] = jnp.max(s, axis=0, keepdims=True)

    def stage_e(i, j):
        for hd in range(2):
            m_prev = m_ref[hd]
            if fox:
                delta = goff_ref[g_base + hd, i] - goff_ref[g_base + hd, j]
                m_new = jnp.maximum(m_prev, mx_ref[hd] + delta)
                shift = m_new - delta
            else:
                m_new = jnp.maximum(m_prev, mx_ref[hd])
                shift = m_new
            p_ref[hd] = jnp.exp2(s_ref[hd] - shift).astype(BF16)
            a_ref[hd] = jnp.exp2(m_prev - m_new)
            m_ref[hd] = m_new

    def stage_p(j):
        for hd in range(2):
            acc_ref[hd] = a_ref[hd] * acc_ref[hd] + jnp.dot(
                vt_ref[0, hd, j], p_ref[hd], preferred_element_type=F32)

    def finish(i):
        outs = []
        for hd in range(2):
            acc = acc_ref[hd]
            outs.append(acc[0:HEAD_V, :] / acc[HEAD_V:HEAD_V + 1, :])
        o_ref[0, pl.ds(pl.multiple_of(i * t, t), t), :] = jnp.transpose(
            jnp.concatenate(outs, axis=0)).astype(o_ref.dtype)
        reset()

    def reset():
        m_ref[...] = jnp.full(m_ref.shape, -jnp.inf, F32)
        acc_ref[...] = jnp.zeros_like(acc_ref)

    reset()
    stage_q(0, 0, True)
    stage_e(0, 0)
    stage_q(1, 0, False)

    def step(prev_diag, next_diag, i, j):
        if prev_diag:
            stage_p(i - 1)
            finish(i - 1)
        else:
            stage_p(j - 1)
        stage_e(i, j)
        cur_diag = j == i
        ni = jnp.where(cur_diag, i + 1, i)
        nj = jnp.where(cur_diag, 0, j + 1)
        stage_q(ni, nj, next_diag)

    def body(_, ij):
        i, j = ij
        prev_diag = j == 0
        next_diag = j + 1 == i

        def variant(pd, nd):
            return lambda: step(pd, nd, i, j)

        lax.cond(prev_diag,
                 lambda: lax.cond(next_diag, variant(True, True), variant(True, False)),
                 lambda: lax.cond(next_diag, variant(False, True), variant(False, False)))
        cur_diag = j == i
        return (jnp.where(cur_diag, i + 1, i), jnp.where(cur_diag, 0, j + 1))

    n_tiles = nblk * (nblk + 1) // 2
    lax.fori_loop(1, n_tiles - 1, body, (jnp.int32(1), jnp.int32(0)))
    stage_p(nblk - 2)
    stage_e(nblk - 1, nblk - 1)
    stage_p(nblk - 1)
    finish(nblk - 1)


def _attn_call(q, k, vt, goff=None):
    B, S, _ = q.shape
    fox = goff is not None
    t = BLK
    nblk = S // t
    assert nblk >= 2
    in_specs = [
        pl.BlockSpec((1, S, 2 * HEAD_PAD), lambda b, p: (b, 0, p)),
        pl.BlockSpec((1, S, 2 * HEAD_PAD), lambda b, p: (b, 0, p)),
        pl.BlockSpec((1, 2, nblk, HEAD_PAD, t), lambda b, p: (b, p, 0, 0, 0)),
    ]
    args = [q, k, vt]
    if fox:
        in_specs = [pl.BlockSpec(memory_space=pltpu.SMEM)] + in_specs
        args = [goff] + args
    stat = pltpu.VMEM((2, 1, t), F32)
    return pl.pallas_call(
        functools.partial(_attn_kernel, fox=fox, nblk=nblk),
        grid=(B, N_PAIRS),
        in_specs=in_specs,
        out_specs=pl.BlockSpec((1, S, 2 * HEAD_V), lambda b, p: (b, 0, p)),
        out_shape=jax.ShapeDtypeStruct((B, S, HEADS * HEAD_V), BF16),
        scratch_shapes=[pltpu.VMEM((2, t, t), F32), pltpu.VMEM((2, t, t), BF16),
                        stat, stat, stat, pltpu.VMEM((2, HEAD_PAD, t), F32)],
        compiler_params=pltpu.CompilerParams(
            dimension_semantics=("parallel", "parallel"),
            vmem_limit_bytes=VMEM_LIMIT),
        name="attn_fox" if fox else "attn_mla",
    )(*args)


def _post_kernel(om_ref, of_ref, ga_ref, gb_ref, x_ref, wom_ref, wof_ref, wout_ref,
                 g1_ref, g2_ref, w1_ref, w2_ref, g3_ref, out_ref):
    y_mla = jnp.dot(om_ref[0], wom_ref[...], preferred_element_type=F32)
    y_fox = jnp.dot(of_ref[0], wof_ref[...], preferred_element_type=F32)
    merged = ga_ref[0].astype(F32) * y_mla + gb_ref[0].astype(F32) * y_fox
    mix = jnp.dot(merged.astype(BF16), wout_ref[...], preferred_element_type=F32)
    x1 = x_ref[0] + _rms(mix, g1_ref[...])
    h2 = _rms(x1, g2_ref[...]).astype(BF16)
    m = jnp.zeros_like(x1)
    for c in range(D_FF // FF_CHUNK):
        sl = slice(c * FF_CHUNK, (c + 1) * FF_CHUNK)
        u = jnp.dot(h2, w1_ref[:, sl], preferred_element_type=F32)
        a = jnp.square(jnp.maximum(u, 0.0)).astype(BF16)
        m = m + jnp.dot(a, w2_ref[sl, :], preferred_element_type=F32)
    out_ref[0] = x1 + _rms(m, g3_ref[...])


def _post_call(o_mla, o_fox, ga, gb, x, wom, wof, wout, g1, g2, w1, w2, g3):
    B, S, D = x.shape
    tm = POST_TM
    tok = lambda n: pl.BlockSpec((1, tm, n), lambda b, j: (b, j, 0))
    in_specs = [tok(512), tok(512), tok(D), tok(D), tok(D)] + [
        _const_spec(a.shape) for a in (wom, wof, wout, g1, g2, w1, w2, g3)]
    return pl.pallas_call(
        _post_kernel,
        grid=(B, S // tm),
        in_specs=in_specs,
        out_specs=tok(D),
        out_shape=jax.ShapeDtypeStruct((B, S, D), F32),
        compiler_params=pltpu.CompilerParams(
            dimension_semantics=("parallel", "parallel"), vmem_limit_bytes=VMEM_LIMIT),
        name="post",
    )(o_mla, o_fox, ga, gb, x, wom, wof, wout, g1, g2, w1, w2, g3)


def _pad_heads(w, width):
    r = w.shape[0]
    w = w.reshape(r, HEADS, width)
    w = jnp.pad(w, ((0, 0), (0, 0), (0, HEAD_PAD - width)))
    return w.reshape(r, HEADS * HEAD_PAD)


def _arrange_in_proj(w, b):
    offs = np.cumsum((0,) + IN_SPLITS)
    cq, ckv, kr, fq, fk, fv, f, ga, gb = [w[:, offs[i]:offs[i + 1]] for i in range(9)]
    bcq, bckv, bkr, bfq, bfk, bfv, bf, bga, bgb = [b[None, offs[i]:offs[i + 1]] for i in range(9)]
    kr_pad = ((0, 0), (MLA_NOPE_DIM, LANES - MLA_NOPE_DIM - MLA_ROPE_DIM))
    f_pad = ((0, 0), (0, LANES - FOX_HEADS))
    w_r = jnp.concatenate([cq, ckv, jnp.pad(kr, kr_pad), _pad_heads(fq, FOX_HEAD_DIM),
                           _pad_heads(fk, FOX_HEAD_DIM), jnp.pad(f, f_pad), ga, gb], axis=1)
    b_r = jnp.concatenate([bcq, bckv, jnp.pad(bkr, kr_pad), _pad_heads(bfq, FOX_HEAD_DIM),
                           _pad_heads(bfk, FOX_HEAD_DIM), jnp.pad(bf, f_pad), bga, bgb], axis=1)
    return w_r.astype(BF16), b_r.astype(F32), fv.T.astype(BF16), bfv.T.astype(F32)


def _static_tables():
    inv = ROPE_THETA ** (-np.arange(HALF_ROPE, dtype=np.float32) / HALF_ROPE)
    freq = np.zeros((1, LANES), np.float32)
    freq[0, MLA_NOPE_DIM:MLA_NOPE_DIM + HALF_ROPE] = inv
    freq[0, MLA_NOPE_DIM + HALF_ROPE:MLA_NOPE_DIM + MLA_ROPE_DIM] = inv
    sgn = np.zeros((2, LANES), np.float32)
    sgn[0, MLA_NOPE_DIM:MLA_NOPE_DIM + HALF_ROPE] = -1.0
    sgn[1, MLA_NOPE_DIM + HALF_ROPE:MLA_NOPE_DIM + MLA_ROPE_DIM] = 1.0
    selq = np.zeros((3, LANES, HEADS * HEAD_PAD), np.float32)
    selk = np.zeros((3, LANES, HEADS * HEAD_PAD), np.float32)
    ones = np.zeros((2, HEADS * HEAD_PAD), np.float32)
    for hd in range(HEADS):
        base = hd * HEAD_PAD + AUG
        for r in range(3):
            selq[r, hd, base + r] = 1.0
            selk[r, hd, base + 3 + r] = -1.0
        ones[0, base + 3:base + 6] = 1.0
        ones[1, base:base + 3] = 1.0
    vfill = np.zeros((HEAD_PAD - HEAD_V, BLK), np.float32)
    vfill[0, :] = 1.0
    return (jnp.asarray(freq), jnp.asarray(sgn), jnp.asarray(selq, BF16), jnp.asarray(selk, BF16),
            jnp.asarray(ones), jnp.asarray(vfill, BF16))


def _layer(x, pos3, ln_pre_mix, ln_post_mix, ln_pre_mlp, ln_post_mlp, w_in, b_in, q_a_norm,
           w_uq, kv_a_norm, w_uk, w_uv, w_o_mla, w_o_fox, w_out, w_ff1, w_ff2):
    B, S, _ = x.shape
    row = lambda v: v[None, :].astype(F32)
    w_r, b_r, wfvt, bfvt = _arrange_in_proj(w_in, b_in)
    freq, sgn, selq, selk, ones, vfill = _static_tables()
    consts = (row(ln_pre_mix), w_r, b_r, row(q_a_norm),
              _pad_heads(w_uq, MLA_NOPE_DIM + MLA_ROPE_DIM).astype(BF16), row(kv_a_norm),
              _pad_heads(w_uk, MLA_NOPE_DIM).astype(BF16), w_uv.T.astype(BF16), wfvt, bfvt,
              freq, sgn, selq, selk, ones, vfill)
    (q_mla, k_mla, vt_mla, q_fox, k_fox, vt_fox, goff, ga, gb) = _pre_call(x, pos3, consts)

    goff = jnp.transpose(goff[:, :, 0, :FOX_HEADS], (0, 2, 1)).reshape(B * FOX_HEADS, S // BLK)

    o_mla = _attn_call(q_mla, k_mla, vt_mla)
    o_fox = _attn_call(q_fox, k_fox, vt_fox, goff)

    return _post_call(o_mla, o_fox, ga, gb, x, w_o_mla.astype(BF16), w_o_fox.astype(BF16),
                      w_out.astype(BF16), row(ln_post_mix), row(ln_pre_mlp),
                      w_ff1.astype(BF16), w_ff2.astype(BF16), row(ln_post_mlp))


def kernel(x, positions, ln_pre_mix, ln_post_mix, ln_pre_mlp, ln_post_mlp, w_in, b_in,
           q_a_norm, w_uq, kv_a_norm, w_uk, w_uv, w_o_mla, w_o_fox, w_out, w_ff1, w_ff2):
    pos3 = positions[:, :, None]
    for l in range(w_in.shape[0]):
        x = _layer(x, pos3, ln_pre_mix[l], ln_post_mix[l], ln_pre_mlp[l], ln_post_mlp[l],
                   w_in[l], b_in[l], q_a_norm[l], w_uq[l], kv_a_norm[l], w_uk[l], w_uv[l],
                   w_o_mla[l], w_o_fox[l], w_out[l], w_ff1[l], w_ff2[l])
    return x
```

```python
import functools

import numpy as np
import jax
import jax.numpy as jnp
from jax import lax
from jax.experimental import pallas as pl
from jax.experimental.pallas import tpu as pltpu

F32 = jnp.float32
BF16 = jnp.bfloat16

D_MODEL = 1024
MLA_HEADS = 8
MLA_Q_LORA = 256
MLA_KV_LORA = 128
MLA_NOPE_DIM = 64
MLA_ROPE_DIM = 32
MLA_V_DIM = 64
FOX_HEADS = 8
FOX_HEAD_DIM = 64
D_FF = 4 * D_MODEL
ROPE_THETA = 10000.0
NORM_EPS = 1e-6
IN_SPLITS = (MLA_Q_LORA, MLA_KV_LORA, MLA_ROPE_DIM,
             FOX_HEADS * FOX_HEAD_DIM, FOX_HEADS * FOX_HEAD_DIM, FOX_HEADS * FOX_HEAD_DIM,
             FOX_HEADS, D_MODEL, D_MODEL)

LANES = 128
HEAD_PAD = 128
HEADS = 8
HEAD_V = 64
HALF_ROPE = MLA_ROPE_DIM // 2
N_PAIRS = HEADS // 2
AUG = 64
N_PIECES = 3
LOG2E = float(np.log2(np.e))
VMEM_LIMIT = 52 * 1024 * 1024

BLK = 512
POST_TM = 512
FF_CHUNK = 512

_W = (MLA_Q_LORA, MLA_KV_LORA, LANES, HEADS * HEAD_PAD, HEADS * HEAD_PAD, LANES, D_MODEL, D_MODEL)
_OFF = tuple(int(v) for v in np.cumsum((0,) + _W))
(C_CQ, C_CKV, C_KR, C_FQ, C_FK, C_F, C_GA, C_GB) = tuple(
    (_OFF[i], _OFF[i + 1]) for i in range(8))


def _rms(x, g):
    return x * lax.rsqrt(jnp.mean(x * x, axis=-1, keepdims=True) + NORM_EPS) * g


def _sigmoid(x):
    return 0.5 * jnp.tanh(0.5 * x) + 0.5


def _const_spec(shape):
    zeros = (0,) * len(shape)
    return pl.BlockSpec(shape, lambda *_: zeros, pipeline_mode=pl.Buffered(1))


def _dot_nt(a, b):
    return lax.dot_general(a, b, (((1,), (1,)), ((), ())), preferred_element_type=F32)


def _pre_kernel(x_ref, pos_ref, g_ref, w_ref, b_ref, qn_ref, wuq_ref, kvn_ref, wuk_ref,
                wuvt_ref, wfvt_ref, bfvt_ref, freq_ref, selq_ref, selk_ref, one_ref, vfill_ref,
                qm_ref, km_ref, vmt_ref, qf_ref, kf_ref, vft_ref, goff_ref, ga_ref, gb_ref,
                carry_ref):
    tm = x_ref.shape[1]
    lane = lax.broadcasted_iota(jnp.int32, (tm, LANES), 1)

    @pl.when(pl.program_id(1) == 0)
    def _():
        carry_ref[...] = jnp.zeros_like(carry_ref)

    h = _rms(x_ref[0], g_ref[...]).astype(BF16)

    def proj(c):
        return (jnp.dot(h, w_ref[:, c[0]:c[1]], preferred_element_type=F32)
                + b_ref[:, c[0]:c[1]])

    ang = freq_ref[...] * pos_ref[0].astype(F32)
    cos_t = jnp.cos(ang)
    sin_t = jnp.sin(ang)
    below = (MLA_NOPE_DIM, tm)
    above = (LANES - MLA_NOPE_DIM - MLA_ROPE_DIM, tm)
    cos = jnp.transpose(jnp.concatenate(
        [jnp.ones(below, F32), cos_t, cos_t, jnp.ones(above, F32)], axis=0))
    sin = jnp.transpose(jnp.concatenate(
        [jnp.zeros(below, F32), -sin_t, sin_t, jnp.zeros(above, F32)], axis=0))
    s_lo = jnp.where(lane < MLA_NOPE_DIM + HALF_ROPE, sin, 0.0)
    s_hi = jnp.where(lane >= MLA_NOPE_DIM + HALF_ROPE, sin, 0.0)

    def rope(blk):
        return (blk * cos + pltpu.roll(blk, LANES - HALF_ROPE, 1) * s_lo
                + pltpu.roll(blk, HALF_ROPE, 1) * s_hi)

    c_q = _rms(proj(C_CQ), qn_ref[...]).astype(BF16)
    q_full = jnp.dot(c_q, wuq_ref[...], preferred_element_type=F32)
    q_scale = (MLA_NOPE_DIM + MLA_ROPE_DIM) ** -0.5 * LOG2E
    c_kv = _rms(proj(C_CKV), kvn_ref[...]).astype(BF16)
    k_nope = jnp.dot(c_kv, wuk_ref[...], preferred_element_type=F32)
    k_rot = rope(proj(C_KR))
    for hd in range(HEADS):
        sl = slice(hd * HEAD_PAD, (hd + 1) * HEAD_PAD)
        qm_ref[0, :, sl] = (rope(q_full[:, sl]) * q_scale).astype(BF16)
        km_ref[0, :, sl] = (k_nope[:, sl] + k_rot).astype(BF16)
    vmt = _dot_nt(wuvt_ref[...], c_kv).astype(BF16)

    def pieces_by_lane_group(v):
        p0 = v.astype(BF16)
        r1 = v - p0.astype(F32)
        p1 = r1.astype(BF16)
        p2 = (r1 - p1.astype(F32)).astype(BF16)
        return jnp.where(lane < FOX_HEADS, p0, jnp.where(lane < 2 * FOX_HEADS, p1, p2))

    f_logit = proj(C_F)
    log_f = jnp.minimum(f_logit, 0.0) - jnp.log1p(jnp.exp(-jnp.abs(f_logit)))
    row = lax.broadcasted_iota(jnp.int32, (tm, tm), 0)
    col = lax.broadcasted_iota(jnp.int32, (tm, tm), 1)
    tri = (row >= col).astype(BF16)
    cum = jnp.dot(tri, pieces_by_lane_group(log_f), preferred_element_type=F32)
    cum = cum + pltpu.roll(cum, LANES - FOX_HEADS, 1) + pltpu.roll(cum, LANES - 2 * FOX_HEADS, 1)
    local = jnp.where(lane < FOX_HEADS, cum, 0.0)
    goff_ref[0, 0] = carry_ref[...] * LOG2E
    carry_ref[...] = carry_ref[...] + local[tm - 1:tm, :]

    local3 = local + pltpu.roll(local, FOX_HEADS, 1) + pltpu.roll(local, 2 * FOX_HEADS, 1)
    aug = pieces_by_lane_group(local3 * LOG2E)
    q_aug = one_ref[0:1, :] + jnp.dot(aug, selq_ref[...], preferred_element_type=F32)
    k_aug = one_ref[1:2, :] + jnp.dot(aug, selk_ref[...], preferred_element_type=F32)
    qf_ref[0] = (proj(C_FQ) * (FOX_HEAD_DIM ** -0.5 * LOG2E) + q_aug).astype(BF16)
    kf_ref[0] = (proj(C_FK) + k_aug).astype(BF16)
    vft = (_dot_nt(wfvt_ref[...], h) + bfvt_ref[...]).astype(BF16)

    vfill = vfill_ref[...]
    for hd in range(HEADS):
        rows = slice(hd * HEAD_V, (hd + 1) * HEAD_V)
        vmt_ref[0, hd, 0, 0:HEAD_V, :] = vmt[rows, :]
        vmt_ref[0, hd, 0, HEAD_V:, :] = vfill
        vft_ref[0, hd, 0, 0:HEAD_V, :] = vft[rows, :]
        vft_ref[0, hd, 0, HEAD_V:, :] = vfill

    ga_ref[0] = _sigmoid(proj(C_GA)).astype(BF16)
    gb_ref[0] = _sigmoid(proj(C_GB)).astype(BF16)


def _pre_call(x, pos_row, consts):
    B, S, D = x.shape
    tm = BLK
    nblk = S // tm
    tok = lambda n: pl.BlockSpec((1, tm, n), lambda b, j: (b, j, 0))
    vt_shape = jax.ShapeDtypeStruct((B, HEADS, nblk, HEAD_PAD, tm), BF16)
    vt_spec = pl.BlockSpec((1, HEADS, 1, HEAD_PAD, tm), lambda b, j: (b, 0, j, 0, 0))
    out_shape = [
        jax.ShapeDtypeStruct((B, S, HEADS * HEAD_PAD), BF16),
        jax.ShapeDtypeStruct((B, S, HEADS * HEAD_PAD), BF16),
        vt_shape,
        jax.ShapeDtypeStruct((B, S, HEADS * HEAD_PAD), BF16),
        jax.ShapeDtypeStruct((B, S, HEADS * HEAD_PAD), BF16),
        vt_shape,
        jax.ShapeDtypeStruct((B, nblk, 1, LANES), F32),
        jax.ShapeDtypeStruct((B, S, D), BF16),
        jax.ShapeDtypeStruct((B, S, D), BF16),
    ]
    out_specs = [tok(HEADS * HEAD_PAD), tok(HEADS * HEAD_PAD), vt_spec,
                 tok(HEADS * HEAD_PAD), tok(HEADS * HEAD_PAD), vt_spec,
                 pl.BlockSpec((1, 1, 1, LANES), lambda b, j: (b, j, 0, 0)),
                 tok(D), tok(D)]
    in_specs = ([tok(D), pl.BlockSpec((1, 1, tm), lambda b, j: (b, 0, j))]
                + [_const_spec(a.shape) for a in consts])
    return pl.pallas_call(
        _pre_kernel,
        grid=(B, nblk),
        in_specs=in_specs,
        out_specs=out_specs,
        out_shape=out_shape,
        scratch_shapes=[pltpu.VMEM((1, LANES), F32)],
        compiler_params=pltpu.CompilerParams(
            dimension_semantics=("parallel", "arbitrary"), vmem_limit_bytes=VMEM_LIMIT),
        name="pre",
    )(x, pos_row, *consts)


def _attn_kernel(*refs, fox, nblk):
    if fox:
        goff_ref, q_ref, k_ref, vt_ref, o_ref, s_ref, p_ref, mx_ref, m_ref, a_ref, acc_ref = refs
    else:
        q_ref, k_ref, vt_ref, o_ref, s_ref, p_ref, mx_ref, m_ref, a_ref, acc_ref = refs
    t = BLK
    g_base = (pl.program_id(0) * N_PAIRS + pl.program_id(1)) * 2

    def stage_q(i, j, diag):
        qoff = pl.multiple_of(i * t, t)
        koff = pl.multiple_of(j * t, t)
        for hd in range(2):
            lanes = slice(hd * HEAD_PAD, (hd + 1) * HEAD_PAD)
            s = _dot_nt(k_ref[0, pl.ds(koff, t), lanes], q_ref[0, pl.ds(qoff, t), lanes])
            if diag:
                key = lax.broadcasted_iota(jnp.int32, (t, t), 0)
                qry = lax.broadcasted_iota(jnp.int32, (t, t), 1)
                s = jnp.where(key <= qry, s, -jnp.inf)
            s_ref[hd] = s
            mx_ref[hd] = jnp.max(s, axis=0, keepdims=True)

    def stage_e(i, j):
        for hd in range(2):
            m_prev = m_ref[hd]
            if fox:
                delta = goff_ref[g_base + hd, i] - goff_ref[g_base + hd, j]
                m_new = jnp.maximum(m_prev, mx_ref[hd] + delta)
                shift = m_new - delta
            else:
                m_new = jnp.maximum(m_prev, mx_ref[hd])
                shift = m_new
            p_ref[hd] = jnp.exp2(s_ref[hd] - shift).astype(BF16)
            a_ref[hd] = jnp.exp2(m_prev - m_new)
            m_ref[hd] = m_new

    def stage_p(j):
        for hd in range(2):
            acc_ref[hd] = a_ref[hd] * acc_ref[hd] + jnp.dot(
                vt_ref[0, hd, j], p_ref[hd], preferred_element_type=F32)

    def finish(i):
        outs = []
        for hd in range(2):
            acc = acc_ref[hd]
            outs.append(acc[0:HEAD_V, :] / acc[HEAD_V:HEAD_V + 1, :])
        o_ref[0, pl.ds(pl.multiple_of(i * t, t), t), :] = jnp.transpose(
            jnp.concatenate(outs, axis=0)).astype(o_ref.dtype)
        reset()

    def reset():
        m_ref[...] = jnp.full(m_ref.shape, -jnp.inf, F32)
        acc_ref[...] = jnp.zeros_like(acc_ref)

    reset()
    stage_q(0, 0, True)
    stage_e(0, 0)
    stage_q(1, 0, False)

    def step(prev_diag, next_diag, i, j):
        if prev_diag:
            stage_p(i - 1)
            finish(i - 1)
        else:
            stage_p(j - 1)
        stage_e(i, j)
        cur_diag = j == i
        ni = jnp.where(cur_diag, i + 1, i)
        nj = jnp.where(cur_diag, 0, j + 1)
        stage_q(ni, nj, next_diag)

    def body(_, ij):
        i, j = ij
        prev_diag = j == 0
        next_diag = j + 1 == i

        def variant(pd, nd):
            return lambda: step(pd, nd, i, j)

        lax.cond(prev_diag,
                 lambda: lax.cond(next_diag, variant(True, True), variant(True, False)),
                 lambda: lax.cond(next_diag, variant(False, True), variant(False, False)))
        cur_diag = j == i
        return (jnp.where(cur_diag, i + 1, i), jnp.where(cur_diag, 0, j + 1))

    n_tiles = nblk * (nblk + 1) // 2
    lax.fori_loop(1, n_tiles - 1, body, (jnp.int32(1), jnp.int32(0)))
    stage_p(nblk - 2)
    stage_e(nblk - 1, nblk - 1)
    stage_p(nblk - 1)
    finish(nblk - 1)


def _attn_call(q, k, vt, goff=None):
    B, S, _ = q.shape
    fox = goff is not None
    t = BLK
    nblk = S // t
    assert nblk >= 2
    in_specs = [
        pl.BlockSpec((1, S, 2 * HEAD_PAD), lambda b, p: (b, 0, p)),
        pl.BlockSpec((1, S, 2 * HEAD_PAD), lambda b, p: (b, 0, p)),
        pl.BlockSpec((1, 2, nblk, HEAD_PAD, t), lambda b, p: (b, p, 0, 0, 0)),
    ]
    args = [q, k, vt]
    if fox:
        in_specs = [pl.BlockSpec(memory_space=pltpu.SMEM)] + in_specs
        args = [goff] + args
    stat = pltpu.VMEM((2, 1, t), F32)
    return pl.pallas_call(
        functools.partial(_attn_kernel, fox=fox, nblk=nblk),
        grid=(B, N_PAIRS),
        in_specs=in_specs,
        out_specs=pl.BlockSpec((1, S, 2 * HEAD_V), lambda b, p: (b, 0, p)),
        out_shape=jax.ShapeDtypeStruct((B, S, HEADS * HEAD_V), BF16),
        scratch_shapes=[pltpu.VMEM((2, t, t), F32), pltpu.VMEM((2, t, t), BF16),
                        stat, stat, stat, pltpu.VMEM((2, HEAD_PAD, t), F32)],
        compiler_params=pltpu.CompilerParams(
            dimension_semantics=("parallel", "parallel"),
            vmem_limit_bytes=VMEM_LIMIT),
        name="attn_fox" if fox else "attn_mla",
    )(*args)


def _post_kernel(om_ref, of_ref, ga_ref, gb_ref, x_ref, wom_ref, wof_ref, wout_ref,
                 g1_ref, g2_ref, w1_ref, w2_ref, g3_ref, out_ref):
    y_mla = jnp.dot(om_ref[0], wom_ref[...], preferred_element_type=F32)
    y_fox = jnp.dot(of_ref[0], wof_ref[...], preferred_element_type=F32)
    merged = ga_ref[0].astype(F32) * y_mla + gb_ref[0].astype(F32) * y_fox
    mix = jnp.dot(merged.astype(BF16), wout_ref[...], preferred_element_type=F32)
    x1 = x_ref[0] + _rms(mix, g1_ref[...])
    h2 = _rms(x1, g2_ref[...]).astype(BF16)
    m = jnp.zeros_like(x1)
    for c in range(D_FF // FF_CHUNK):
        sl = slice(c * FF_CHUNK, (c + 1) * FF_CHUNK)
        u = jnp.dot(h2, w1_ref[:, sl], preferred_element_type=F32)
        a = jnp.square(jnp.maximum(u, 0.0)).astype(BF16)
        m = m + jnp.dot(a, w2_ref[sl, :], preferred_element_type=F32)
    out_ref[0] = x1 + _rms(m, g3_ref[...])


def _post_call(o_mla, o_fox, ga, gb, x, wom, wof, wout, g1, g2, w1, w2, g3):
    B, S, D = x.shape
    tm = POST_TM
    tok = lambda n: pl.BlockSpec((1, tm, n), lambda b, j: (b, j, 0))
    in_specs = [tok(512), tok(512), tok(D), tok(D), tok(D)] + [
        _const_spec(a.shape) for a in (wom, wof, wout, g1, g2, w1, w2, g3)]
    return pl.pallas_call(
        _post_kernel,
        grid=(B, S // tm),
        in_specs=in_specs,
        out_specs=tok(D),
        out_shape=jax.ShapeDtypeStruct((B, S, D), F32),
        compiler_params=pltpu.CompilerParams(
            dimension_semantics=("parallel", "parallel"), vmem_limit_bytes=VMEM_LIMIT),
        name="post",
    )(o_mla, o_fox, ga, gb, x, wom, wof, wout, g1, g2, w1, w2, g3)


def _pad_heads(w, width):
    r = w.shape[0]
    w = w.reshape(r, HEADS, width)
    w = jnp.pad(w, ((0, 0), (0, 0), (0, HEAD_PAD - width)))
    return w.reshape(r, HEADS * HEAD_PAD)


def _arrange_in_proj(w, b):
    offs = np.cumsum((0,) + IN_SPLITS)
    cq, ckv, kr, fq, fk, fv, f, ga, gb = [w[:, offs[i]:offs[i + 1]] for i in range(9)]
    bcq, bckv, bkr, bfq, bfk, bfv, bf, bga, bgb = [b[None, offs[i]:offs[i + 1]] for i in range(9)]
    kr_pad = ((0, 0), (MLA_NOPE_DIM, LANES - MLA_NOPE_DIM - MLA_ROPE_DIM))
    f_rep = lambda a: jnp.pad(jnp.tile(a, (1, N_PIECES)),
                              ((0, 0), (0, LANES - N_PIECES * FOX_HEADS)))
    w_r = jnp.concatenate([cq, ckv, jnp.pad(kr, kr_pad), _pad_heads(fq, FOX_HEAD_DIM),
                           _pad_heads(fk, FOX_HEAD_DIM), f_rep(f), ga, gb], axis=1)
    b_r = jnp.concatenate([bcq, bckv, jnp.pad(bkr, kr_pad), _pad_heads(bfq, FOX_HEAD_DIM),
                           _pad_heads(bfk, FOX_HEAD_DIM), f_rep(bf), bga, bgb], axis=1)
    return w_r.astype(BF16), b_r.astype(F32), fv.T.astype(BF16), bfv.T.astype(F32)


def _static_tables():
    inv = ROPE_THETA ** (-np.arange(HALF_ROPE, dtype=np.float32) / HALF_ROPE)
    freq = inv.reshape(HALF_ROPE, 1)
    selq = np.zeros((LANES, HEADS * HEAD_PAD), np.float32)
    selk = np.zeros((LANES, HEADS * HEAD_PAD), np.float32)
    ones = np.zeros((2, HEADS * HEAD_PAD), np.float32)
    for hd in range(HEADS):
        base = hd * HEAD_PAD + AUG
        for r in range(N_PIECES):
            selq[FOX_HEADS * r + hd, base + r] = 1.0
            selk[FOX_HEADS * r + hd, base + N_PIECES + r] = -1.0
        ones[0, base + N_PIECES:base + 2 * N_PIECES] = 1.0
        ones[1, base:base + N_PIECES] = 1.0
    vfill = np.zeros((HEAD_PAD - HEAD_V, BLK), np.float32)
    vfill[0, :] = 1.0
    return (jnp.asarray(freq), jnp.asarray(selq, BF16), jnp.asarray(selk, BF16),
            jnp.asarray(ones), jnp.asarray(vfill, BF16))


def _layer(x, pos_row, ln_pre_mix, ln_post_mix, ln_pre_mlp, ln_post_mlp, w_in, b_in, q_a_norm,
           w_uq, kv_a_norm, w_uk, w_uv, w_o_mla, w_o_fox, w_out, w_ff1, w_ff2):
    B, S, _ = x.shape
    row = lambda v: v[None, :].astype(F32)
    w_r, b_r, wfvt, bfvt = _arrange_in_proj(w_in, b_in)
    freq, selq, selk, ones, vfill = _static_tables()
    consts = (row(ln_pre_mix), w_r, b_r, row(q_a_norm),
              _pad_heads(w_uq, MLA_NOPE_DIM + MLA_ROPE_DIM).astype(BF16), row(kv_a_norm),
              _pad_heads(w_uk, MLA_NOPE_DIM).astype(BF16), w_uv.T.astype(BF16), wfvt, bfvt,
              freq, selq, selk, ones, vfill)
    (q_mla, k_mla, vt_mla, q_fox, k_fox, vt_fox, goff, ga, gb) = _pre_call(x, pos_row, consts)

    goff = jnp.transpose(goff[:, :, 0, :FOX_HEADS], (0, 2, 1)).reshape(B * FOX_HEADS, S // BLK)

    o_mla = _attn_call(q_mla, k_mla, vt_mla)
    o_fox = _attn_call(q_fox, k_fox, vt_fox, goff)

    return _post_call(o_mla, o_fox, ga, gb, x, w_o_mla.astype(BF16), w_o_fox.astype(BF16),
                      w_out.astype(BF16), row(ln_post_mix), row(ln_pre_mlp),
                      w_ff1.astype(BF16), w_ff2.astype(BF16), row(ln_post_mlp))


def kernel(x, positions, ln_pre_mix, ln_post_mix, ln_pre_mlp, ln_post_mlp, w_in, b_in,
           q_a_norm, w_uq, kv_a_norm, w_uk, w_uv, w_o_mla, w_o_fox, w_out, w_ff1, w_ff2):
    pos_row = positions[:, None, :]
    for l in range(w_in.shape[0]):
        x = _layer(x, pos_row, ln_pre_mix[l], ln_post_mix[l], ln_pre_mlp[l], ln_post_mlp[l],
                   w_in[l], b_in[l], q_a_norm[l], w_uq[l], kv_a_norm[l], w_uk[l], w_uv[l],
                   w_o_mla[l], w_o_fox[l], w_out[l], w_ff1[l], w_ff2[l])
    return x
```

```python
import functools

import numpy as np
import jax
import jax.numpy as jnp
from jax import lax
from jax.experimental import pallas as pl
from jax.experimental.pallas import tpu as pltpu

F32 = jnp.float32
BF16 = jnp.bfloat16

D_MODEL = 1024
MLA_HEADS = 8
MLA_Q_LORA = 256
MLA_KV_LORA = 128
MLA_NOPE_DIM = 64
MLA_ROPE_DIM = 32
MLA_V_DIM = 64
FOX_HEADS = 8
FOX_HEAD_DIM = 64
D_FF = 4 * D_MODEL
ROPE_THETA = 10000.0
NORM_EPS = 1e-6
IN_SPLITS = (MLA_Q_LORA, MLA_KV_LORA, MLA_ROPE_DIM,
             FOX_HEADS * FOX_HEAD_DIM, FOX_HEADS * FOX_HEAD_DIM, FOX_HEADS * FOX_HEAD_DIM,
             FOX_HEADS, D_MODEL, D_MODEL)

LANES = 128
HEAD_PAD = 128
HEADS = 8
HEAD_V = 64
HALF_ROPE = MLA_ROPE_DIM // 2
N_PAIRS = HEADS // 2
AUG = 64
N_PIECES = 3
PIECE_LANES = N_PIECES * FOX_HEADS
LOG2E = float(np.log2(np.e))
VMEM_LIMIT = 52 * 1024 * 1024

BLK = 512
POST_TM = 512
FF_CHUNK = 512

_W = (MLA_Q_LORA, MLA_KV_LORA, LANES, FOX_HEADS * FOX_HEAD_DIM, FOX_HEADS * FOX_HEAD_DIM, LANES,
      D_MODEL, D_MODEL)
_OFF = tuple(int(v) for v in np.cumsum((0,) + _W))
(C_CQ, C_CKV, C_KR, C_FQ, C_FK, C_F, C_GA, C_GB) = tuple(
    (_OFF[i], _OFF[i + 1]) for i in range(8))


def _rms(x, g):
    return x * lax.rsqrt(jnp.mean(x * x, axis=-1, keepdims=True) + NORM_EPS) * g


def _sigmoid(x):
    return 0.5 * jnp.tanh(0.5 * x) + 0.5


def _const_spec(shape):
    zeros = (0,) * len(shape)
    return pl.BlockSpec(shape, lambda *_: zeros, pipeline_mode=pl.Buffered(1))


def _dot_nt(a, b):
    return lax.dot_general(a, b, (((1,), (1,)), ((), ())), preferred_element_type=F32)


def _pre_kernel(x_ref, pos_ref, g_ref, w_ref, b_ref, qn_ref, wuq_ref, kvn_ref, wuk_ref,
                wuvt_ref, wfvt_ref, bfvt_ref, freq_ref, piece_ref, vfill_ref,
                qm_ref, km_ref, vmt_ref, qf_ref, kf_ref, vft_ref, goff_ref, ga_ref, gb_ref,
                carry_ref):
    tm = x_ref.shape[1]
    lane = lax.broadcasted_iota(jnp.int32, (tm, LANES), 1)

    @pl.when(pl.program_id(1) == 0)
    def _():
        carry_ref[...] = jnp.zeros_like(carry_ref)

    h = _rms(x_ref[0], g_ref[...]).astype(BF16)

    def proj(c):
        return (jnp.dot(h, w_ref[:, c[0]:c[1]], preferred_element_type=F32)
                + b_ref[:, c[0]:c[1]])

    ang = freq_ref[...] * pos_ref[0].astype(F32)
    cos_t = jnp.cos(ang)
    sin_t = jnp.sin(ang)
    below = (MLA_NOPE_DIM, tm)
    above = (LANES - MLA_NOPE_DIM - MLA_ROPE_DIM, tm)
    cos = jnp.transpose(jnp.concatenate(
        [jnp.ones(below, F32), cos_t, cos_t, jnp.ones(above, F32)], axis=0))
    sin = jnp.transpose(jnp.concatenate(
        [jnp.zeros(below, F32), -sin_t, sin_t, jnp.zeros(above, F32)], axis=0))
    s_lo = jnp.where(lane < MLA_NOPE_DIM + HALF_ROPE, sin, 0.0)
    s_hi = jnp.where(lane >= MLA_NOPE_DIM + HALF_ROPE, sin, 0.0)

    def rope(blk):
        return (blk * cos + pltpu.roll(blk, LANES - HALF_ROPE, 1) * s_lo
                + pltpu.roll(blk, HALF_ROPE, 1) * s_hi)

    c_q = _rms(proj(C_CQ), qn_ref[...]).astype(BF16)
    q_full = jnp.dot(c_q, wuq_ref[...], preferred_element_type=F32)
    q_scale = (MLA_NOPE_DIM + MLA_ROPE_DIM) ** -0.5 * LOG2E
    c_kv = _rms(proj(C_CKV), kvn_ref[...]).astype(BF16)
    k_nope = jnp.dot(c_kv, wuk_ref[...], preferred_element_type=F32)
    k_rot = rope(proj(C_KR))
    for hd in range(HEADS):
        sl = slice(hd * HEAD_PAD, (hd + 1) * HEAD_PAD)
        qm_ref[0, :, sl] = (rope(q_full[:, sl]) * q_scale).astype(BF16)
        km_ref[0, :, sl] = (k_nope[:, sl] + k_rot).astype(BF16)
    vmt = _dot_nt(wuvt_ref[...], c_kv).astype(BF16)

    piece_id = piece_ref[...]

    def pieces_by_lane(v):
        p0 = v.astype(BF16)
        r1 = v - p0.astype(F32)
        p1 = r1.astype(BF16)
        p2 = (r1 - p1.astype(F32)).astype(BF16)
        return jnp.where(piece_id == 0.0, p0, jnp.where(piece_id == 1.0, p1, p2))

    f_logit = proj(C_F)
    log_f = jnp.minimum(f_logit, 0.0) - jnp.log1p(jnp.exp(-jnp.abs(f_logit)))
    row = lax.broadcasted_iota(jnp.int32, (tm, tm), 0)
    col = lax.broadcasted_iota(jnp.int32, (tm, tm), 1)
    tri = (row >= col).astype(BF16)
    cum = jnp.dot(tri, pieces_by_lane(log_f), preferred_element_type=F32)
    cum = cum + pltpu.roll(cum, LANES - 1, 1) + pltpu.roll(cum, LANES - 2, 1)
    local = jnp.where(piece_id == 0.0, cum, 0.0)
    goff_ref[0, 0] = carry_ref[...] * LOG2E
    carry_ref[...] = carry_ref[...] + local[tm - 1:tm, :]

    local3 = local + pltpu.roll(local, 1, 1) + pltpu.roll(local, 2, 1)
    aug = pieces_by_lane(local3 * LOG2E).astype(F32)

    fq = proj(C_FQ) * (FOX_HEAD_DIM ** -0.5 * LOG2E)
    fk = proj(C_FK)
    feat = lane < FOX_HEAD_DIM
    aug_a = lane < AUG + N_PIECES
    aug_b = lane < AUG + 2 * N_PIECES
    for hd in range(HEADS):
        src = slice((hd // 2) * LANES, (hd // 2 + 1) * LANES)
        dst = slice(hd * HEAD_PAD, (hd + 1) * HEAD_PAD)
        q_blk, k_blk = fq[:, src], fk[:, src]
        if hd % 2:
            q_blk = pltpu.roll(q_blk, FOX_HEAD_DIM, 1)
            k_blk = pltpu.roll(k_blk, FOX_HEAD_DIM, 1)
        a_t = pltpu.roll(aug, AUG - N_PIECES * hd, 1)
        b_s = pltpu.roll(-aug, AUG + N_PIECES - N_PIECES * hd, 1)
        qf_ref[0, :, dst] = jnp.where(
            feat, q_blk, jnp.where(aug_a, a_t, jnp.where(aug_b, 1.0, 0.0))).astype(BF16)
        kf_ref[0, :, dst] = jnp.where(
            feat, k_blk, jnp.where(aug_a, 1.0, jnp.where(aug_b, b_s, 0.0))).astype(BF16)
    vft = (_dot_nt(wfvt_ref[...], h) + bfvt_ref[...]).astype(BF16)

    vfill = vfill_ref[...]
    for hd in range(HEADS):
        rows = slice(hd * HEAD_V, (hd + 1) * HEAD_V)
        vmt_ref[0, hd, 0, 0:HEAD_V, :] = vmt[rows, :]
        vmt_ref[0, hd, 0, HEAD_V:, :] = vfill
        vft_ref[0, hd, 0, 0:HEAD_V, :] = vft[rows, :]
        vft_ref[0, hd, 0, HEAD_V:, :] = vfill

    ga_ref[0] = _sigmoid(proj(C_GA)).astype(BF16)
    gb_ref[0] = _sigmoid(proj(C_GB)).astype(BF16)


def _pre_call(x, pos_row, consts):
    B, S, D = x.shape
    tm = BLK
    nblk = S // tm
    tok = lambda n: pl.BlockSpec((1, tm, n), lambda b, j: (b, j, 0))
    vt_shape = jax.ShapeDtypeStruct((B, HEADS, nblk, HEAD_PAD, tm), BF16)
    vt_spec = pl.BlockSpec((1, HEADS, 1, HEAD_PAD, tm), lambda b, j: (b, 0, j, 0, 0))
    out_shape = [
        jax.ShapeDtypeStruct((B, S, HEADS * HEAD_PAD), BF16),
        jax.ShapeDtypeStruct((B, S, HEADS * HEAD_PAD), BF16),
        vt_shape,
        jax.ShapeDtypeStruct((B, S, HEADS * HEAD_PAD), BF16),
        jax.ShapeDtypeStruct((B, S, HEADS * HEAD_PAD), BF16),
        vt_shape,
        jax.ShapeDtypeStruct((B, nblk, 1, LANES), F32),
        jax.ShapeDtypeStruct((B, S, D), BF16),
        jax.ShapeDtypeStruct((B, S, D), BF16),
    ]
    out_specs = [tok(HEADS * HEAD_PAD), tok(HEADS * HEAD_PAD), vt_spec,
                 tok(HEADS * HEAD_PAD), tok(HEADS * HEAD_PAD), vt_spec,
                 pl.BlockSpec((1, 1, 1, LANES), lambda b, j: (b, j, 0, 0)),
                 tok(D), tok(D)]
    in_specs = ([tok(D), pl.BlockSpec((1, 1, tm), lambda b, j: (b, 0, j))]
                + [_const_spec(a.shape) for a in consts])
    return pl.pallas_call(
        _pre_kernel,
        grid=(B, nblk),
        in_specs=in_specs,
        out_specs=out_specs,
        out_shape=out_shape,
        scratch_shapes=[pltpu.VMEM((1, LANES), F32)],
        compiler_params=pltpu.CompilerParams(
            dimension_semantics=("parallel", "arbitrary"), vmem_limit_bytes=VMEM_LIMIT),
        name="pre",
    )(x, pos_row, *consts)


def _attn_kernel(*refs, fox, nblk):
    if fox:
        goff_ref, q_ref, k_ref, vt_ref, o_ref, s_ref, p_ref, mx_ref, m_ref, a_ref, acc_ref = refs
    else:
        q_ref, k_ref, vt_ref, o_ref, s_ref, p_ref, mx_ref, m_ref, a_ref, acc_ref = refs
    t = BLK
    g_base = (pl.program_id(0) * N_PAIRS + pl.program_id(1)) * 2

    def stage_q(i, j, diag):
        qoff = pl.multiple_of(i * t, t)
        koff = pl.multiple_of(j * t, t)
        for hd in range(2):
            lanes = slice(hd * HEAD_PAD, (hd + 1) * HEAD_PAD)
            s = _dot_nt(k_ref[0, pl.ds(koff, t), lanes], q_ref[0, pl.ds(qoff, t), lanes])
            if diag:
                key = lax.broadcasted_iota(jnp.int32, (t, t), 0)
                qry = lax.broadcasted_iota(jnp.int32, (t, t), 1)
                s = jnp.where(key <= qry, s, -jnp.inf)
            s_ref[hd] = s
            mx_ref[hd] = jnp.max(s, axis=0, keepdims=True)

    def stage_e(i, j):
        for hd in range(2):
            m_prev = m_ref[hd]
            if fox:
                delta = goff_ref[g_base + hd, i] - goff_ref[g_base + hd, j]
                m_new = jnp.maximum(m_prev, mx_ref[hd] + delta)
                shift = m_new - delta
            else:
                m_new = jnp.maximum(m_prev, mx_ref[hd])
                shift = m_new
            p_ref[hd] = jnp.exp2(s_ref[hd] - shift).astype(BF16)
            a_ref[hd] = jnp.exp2(m_prev - m_new)
            m_ref[hd] = m_new

    def stage_p(j):
        for hd in range(2):
            acc_ref[hd] = a_ref[hd] * acc_ref[hd] + jnp.dot(
                vt_ref[0, hd, j], p_ref[hd], preferred_element_type=F32)

    def finish(i):
        outs = []
        for hd in range(2):
            acc = acc_ref[hd]
            outs.append(acc[0:HEAD_V, :] / acc[HEAD_V:HEAD_V + 1, :])
        o_ref[0, pl.ds(pl.multiple_of(i * t, t), t), :] = jnp.transpose(
            jnp.concatenate(outs, axis=0)).astype(o_ref.dtype)
        reset()

    def reset():
        m_ref[...] = jnp.full(m_ref.shape, -jnp.inf, F32)
        acc_ref[...] = jnp.zeros_like(acc_ref)

    reset()
    stage_q(0, 0, True)
    stage_e(0, 0)
    stage_q(1, 0, False)

    def step(prev_diag, next_diag, i, j):
        if prev_diag:
            stage_p(i - 1)
            finish(i - 1)
        else:
            stage_p(j - 1)
        stage_e(i, j)
        cur_diag = j == i
        ni = jnp.where(cur_diag, i + 1, i)
        nj = jnp.where(cur_diag, 0, j + 1)
        stage_q(ni, nj, next_diag)

    def body(_, ij):
        i, j = ij
        prev_diag = j == 0
        next_diag = j + 1 == i

        def variant(pd, nd):
            return lambda: step(pd, nd, i, j)

        lax.cond(prev_diag,
                 lambda: lax.cond(next_diag, variant(True, True), variant(True, False)),
                 lambda: lax.cond(next_diag, variant(False, True), variant(False, False)))
        cur_diag = j == i
        return (jnp.where(cur_diag, i + 1, i), jnp.where(cur_diag, 0, j + 1))

    n_tiles = nblk * (nblk + 1) // 2
    lax.fori_loop(1, n_tiles - 1, body, (jnp.int32(1), jnp.int32(0)))
    stage_p(nblk - 2)
    stage_e(nblk - 1, nblk - 1)
    stage_p(nblk - 1)
    finish(nblk - 1)


def _attn_call(q, k, vt, goff=None):
    B, S, _ = q.shape
    fox = goff is not None
    t = BLK
    nblk = S // t
    assert nblk >= 2
    in_specs = [
        pl.BlockSpec((1, S, 2 * HEAD_PAD), lambda b, p: (b, 0, p)),
        pl.BlockSpec((1, S, 2 * HEAD_PAD), lambda b, p: (b, 0, p)),
        pl.BlockSpec((1, 2, nblk, HEAD_PAD, t), lambda b, p: (b, p, 0, 0, 0)),
    ]
    args = [q, k, vt]
    if fox:
        in_specs = [pl.BlockSpec(memory_space=pltpu.SMEM)] + in_specs
        args = [goff] + args
    stat = pltpu.VMEM((2, 1, t), F32)
    return pl.pallas_call(
        functools.partial(_attn_kernel, fox=fox, nblk=nblk),
        grid=(B, N_PAIRS),
        in_specs=in_specs,
        out_specs=pl.BlockSpec((1, S, 2 * HEAD_V), lambda b, p: (b, 0, p)),
        out_shape=jax.ShapeDtypeStruct((B, S, HEADS * HEAD_V), BF16),
        scratch_shapes=[pltpu.VMEM((2, t, t), F32), pltpu.VMEM((2, t, t), BF16),
                        stat, stat, stat, pltpu.VMEM((2, HEAD_PAD, t), F32)],
        compiler_params=pltpu.CompilerParams(
            dimension_semantics=("parallel", "parallel"),
            vmem_limit_bytes=VMEM_LIMIT),
        name="attn_fox" if fox else "attn_mla",
    )(*args)


def _post_kernel(om_ref, of_ref, ga_ref, gb_ref, x_ref, wom_ref, wof_ref, wout_ref,
                 g1_ref, g2_ref, w1_ref, w2_ref, g3_ref, out_ref):
    y_mla = jnp.dot(om_ref[0], wom_ref[...], preferred_element_type=F32)
    y_fox = jnp.dot(of_ref[0], wof_ref[...], preferred_element_type=F32)
    merged = ga_ref[0].astype(F32) * y_mla + gb_ref[0].astype(F32) * y_fox
    mix = jnp.dot(merged.astype(BF16), wout_ref[...], preferred_element_type=F32)
    x1 = x_ref[0] + _rms(mix, g1_ref[...])
    h2 = _rms(x1, g2_ref[...]).astype(BF16)
    m = jnp.zeros_like(x1)
    for c in range(D_FF // FF_CHUNK):
        sl = slice(c * FF_CHUNK, (c + 1) * FF_CHUNK)
        u = jnp.dot(h2, w1_ref[:, sl], preferred_element_type=F32)
        a = jnp.square(jnp.maximum(u, 0.0)).astype(BF16)
        m = m + jnp.dot(a, w2_ref[sl, :], preferred_element_type=F32)
    out_ref[0] = x1 + _rms(m, g3_ref[...])


def _post_call(o_mla, o_fox, ga, gb, x, wom, wof, wout, g1, g2, w1, w2, g3):
    B, S, D = x.shape
    tm = POST_TM
    tok = lambda n: pl.BlockSpec((1, tm, n), lambda b, j: (b, j, 0))
    in_specs = [tok(512), tok(512), tok(D), tok(D), tok(D)] + [
        _const_spec(a.shape) for a in (wom, wof, wout, g1, g2, w1, w2, g3)]
    return pl.pallas_call(
        _post_kernel,
        grid=(B, S // tm),
        in_specs=in_specs,
        out_specs=tok(D),
        out_shape=jax.ShapeDtypeStruct((B, S, D), F32),
        compiler_params=pltpu.CompilerParams(
            dimension_semantics=("parallel", "parallel"), vmem_limit_bytes=VMEM_LIMIT),
        name="post",
    )(o_mla, o_fox, ga, gb, x, wom, wof, wout, g1, g2, w1, w2, g3)


def _pad_heads(w, width):
    r = w.shape[0]
    w = w.reshape(r, HEADS, width)
    w = jnp.pad(w, ((0, 0), (0, 0), (0, HEAD_PAD - width)))
    return w.reshape(r, HEADS * HEAD_PAD)


def _arrange_in_proj(w, b):
    offs = np.cumsum((0,) + IN_SPLITS)
    cq, ckv, kr, fq, fk, fv, f, ga, gb = [w[:, offs[i]:offs[i + 1]] for i in range(9)]
    bcq, bckv, bkr, bfq, bfk, bfv, bf, bga, bgb = [b[None, offs[i]:offs[i + 1]] for i in range(9)]
    kr_pad = ((0, 0), (MLA_NOPE_DIM, LANES - MLA_NOPE_DIM - MLA_ROPE_DIM))
    f_rep = lambda a: jnp.pad(jnp.repeat(a, N_PIECES, axis=1), ((0, 0), (0, LANES - PIECE_LANES)))
    w_r = jnp.concatenate([cq, ckv, jnp.pad(kr, kr_pad), fq, fk, f_rep(f), ga, gb], axis=1)
    b_r = jnp.concatenate([bcq, bckv, jnp.pad(bkr, kr_pad), bfq, bfk, f_rep(bf), bga, bgb], axis=1)
    return w_r.astype(BF16), b_r.astype(F32), fv.T.astype(BF16), bfv.T.astype(F32)


def _static_tables():
    inv = ROPE_THETA ** (-np.arange(HALF_ROPE, dtype=np.float32) / HALF_ROPE)
    freq = inv.reshape(HALF_ROPE, 1)
    piece = np.full((1, LANES), float(N_PIECES), np.float32)
    piece[0, :PIECE_LANES] = np.arange(PIECE_LANES) % N_PIECES
    vfill = np.zeros((HEAD_PAD - HEAD_V, BLK), np.float32)
    vfill[0, :] = 1.0
    return jnp.asarray(freq), jnp.asarray(piece), jnp.asarray(vfill, BF16)


def _layer(x, pos_row, ln_pre_mix, ln_post_mix, ln_pre_mlp, ln_post_mlp, w_in, b_in, q_a_norm,
           w_uq, kv_a_norm, w_uk, w_uv, w_o_mla, w_o_fox, w_out, w_ff1, w_ff2):
    B, S, _ = x.shape
    row = lambda v: v[None, :].astype(F32)
    w_r, b_r, wfvt, bfvt = _arrange_in_proj(w_in, b_in)
    freq, piece, vfill = _static_tables()
    consts = (row(ln_pre_mix), w_r, b_r, row(q_a_norm),
              _pad_heads(w_uq, MLA_NOPE_DIM + MLA_ROPE_DIM).astype(BF16), row(kv_a_norm),
              _pad_heads(w_uk, MLA_NOPE_DIM).astype(BF16), w_uv.T.astype(BF16), wfvt, bfvt,
              freq, piece, vfill)
    (q_mla, k_mla, vt_mla, q_fox, k_fox, vt_fox, goff, ga, gb) = _pre_call(x, pos_row, consts)

    goff = jnp.transpose(goff[:, :, 0, 0:PIECE_LANES:N_PIECES], (0, 2, 1))
    goff = goff.reshape(B * FOX_HEADS, S // BLK)

    o_mla = _attn_call(q_mla, k_mla, vt_mla)
    o_fox = _attn_call(q_fox, k_fox, vt_fox, goff)

    return _post_call(o_mla, o_fox, ga, gb, x, w_o_mla.astype(BF16), w_o_fox.astype(BF16),
                      w_out.astype(BF16), row(ln_post_mix), row(ln_pre_mlp),
                      w_ff1.astype(BF16), w_ff2.astype(BF16), row(ln_post_mlp))


def kernel(x, positions, ln_pre_mix, ln_post_mix, ln_pre_mlp, ln_post_mlp, w_in, b_in,
           q_a_norm, w_uq, kv_a_norm, w_uk, w_uv, w_o_mla, w_o_fox, w_out, w_ff1, w_ff2):
    pos_row = positions[:, None, :]
    for l in range(w_in.shape[0]):
        x = _layer(x, pos_row, ln_pre_mix[l], ln_post_mix[l], ln_pre_mlp[l], ln_post_mlp[l],
                   w_in[l], b_in[l], q_a_norm[l], w_uq[l], kv_a_norm[l], w_uk[l], w_uv[l],
                   w_o_mla[l], w_o_fox[l], w_out[l], w_ff1[l], w_ff2[l])
    return x
```

```python
import functools

import numpy as np
import jax
import jax.numpy as jnp
from jax import lax
from jax.experimental import pallas as pl
from jax.experimental.pallas import tpu as pltpu

F32 = jnp.float32
BF16 = jnp.bfloat16

D_MODEL = 1024
MLA_HEADS = 8
MLA_Q_LORA = 256
MLA_KV_LORA = 128
MLA_NOPE_DIM = 64
MLA_ROPE_DIM = 32
MLA_V_DIM = 64
FOX_HEADS = 8
FOX_HEAD_DIM = 64
D_FF = 4 * D_MODEL
ROPE_THETA = 10000.0
NORM_EPS = 1e-6
IN_SPLITS = (MLA_Q_LORA, MLA_KV_LORA, MLA_ROPE_DIM,
             FOX_HEADS * FOX_HEAD_DIM, FOX_HEADS * FOX_HEAD_DIM, FOX_HEADS * FOX_HEAD_DIM,
             FOX_HEADS, D_MODEL, D_MODEL)

LANES = 128
HEAD_PAD = 128
HEADS = 8
HEAD_V = 64
HALF_ROPE = MLA_ROPE_DIM // 2
N_PAIRS = HEADS // 2
AUG = 64
N_PIECES = 3
PIECE_LANES = N_PIECES * FOX_HEADS
LOG2E = float(np.log2(np.e))
VMEM_LIMIT = 52 * 1024 * 1024

BLK = 512
POST_TM = 512
FF_CHUNK = 512

_W = (MLA_Q_LORA, MLA_KV_LORA, LANES, FOX_HEADS * FOX_HEAD_DIM, FOX_HEADS * FOX_HEAD_DIM, LANES,
      D_MODEL, D_MODEL)
_OFF = tuple(int(v) for v in np.cumsum((0,) + _W))
(C_CQ, C_CKV, C_KR, C_FQ, C_FK, C_F, C_GA, C_GB) = tuple(
    (_OFF[i], _OFF[i + 1]) for i in range(8))


def _rms(x, g):
    return x * lax.rsqrt(jnp.mean(x * x, axis=-1, keepdims=True) + NORM_EPS) * g


def _sigmoid(x):
    return 0.5 * jnp.tanh(0.5 * x) + 0.5


def _const_spec(shape):
    zeros = (0,) * len(shape)
    return pl.BlockSpec(shape, lambda *_: zeros, pipeline_mode=pl.Buffered(1))


def _dot_nt(a, b):
    return lax.dot_general(a, b, (((1,), (1,)), ((), ())), preferred_element_type=F32)


def _pre_kernel(x_ref, pos_ref, g_ref, w_ref, b_ref, qn_ref, wuq_ref, kvn_ref, wuk_ref,
                wuvt_ref, wfvt_ref, bfvt_ref, freq_ref, piece_ref, vfill_ref,
                qm_ref, km_ref, vmt_ref, qf_ref, kf_ref, vft_ref, goff_ref, ga_ref, gb_ref,
                carry_ref):
    tm = x_ref.shape[1]
    lane = lax.broadcasted_iota(jnp.int32, (tm, LANES), 1)

    @pl.when(pl.program_id(1) == 0)
    def _():
        carry_ref[...] = jnp.zeros_like(carry_ref)

    h = _rms(x_ref[0], g_ref[...]).astype(BF16)

    def proj(c):
        return (jnp.dot(h, w_ref[:, c[0]:c[1]], preferred_element_type=F32)
                + b_ref[:, c[0]:c[1]])

    ang = freq_ref[...] * pos_ref[0].astype(F32)
    cos_t = jnp.cos(ang)
    sin_t = jnp.sin(ang)
    below = (MLA_NOPE_DIM, tm)
    above = (LANES - MLA_NOPE_DIM - MLA_ROPE_DIM, tm)
    cos = jnp.transpose(jnp.concatenate(
        [jnp.ones(below, F32), cos_t, cos_t, jnp.ones(above, F32)], axis=0))
    sin = jnp.transpose(jnp.concatenate(
        [jnp.zeros(below, F32), -sin_t, sin_t, jnp.zeros(above, F32)], axis=0))
    s_lo = jnp.where(lane < MLA_NOPE_DIM + HALF_ROPE, sin, 0.0)
    s_hi = jnp.where(lane >= MLA_NOPE_DIM + HALF_ROPE, sin, 0.0)

    def rope(blk):
        return (blk * cos + pltpu.roll(blk, LANES - HALF_ROPE, 1) * s_lo
                + pltpu.roll(blk, HALF_ROPE, 1) * s_hi)

    c_q = _rms(proj(C_CQ), qn_ref[...]).astype(BF16)
    q_full = jnp.dot(c_q, wuq_ref[...], preferred_element_type=F32)
    q_scale = (MLA_NOPE_DIM + MLA_ROPE_DIM) ** -0.5 * LOG2E
    c_kv = _rms(proj(C_CKV), kvn_ref[...]).astype(BF16)
    k_nope = jnp.dot(c_kv, wuk_ref[...], preferred_element_type=F32)
    k_rot = rope(proj(C_KR))
    for hd in range(HEADS):
        sl = slice(hd * HEAD_PAD, (hd + 1) * HEAD_PAD)
        qm_ref[0, :, sl] = (rope(q_full[:, sl]) * q_scale).astype(BF16)
        km_ref[0, :, sl] = (k_nope[:, sl] + k_rot).astype(BF16)
    vmt = _dot_nt(wuvt_ref[...], c_kv).astype(BF16)

    piece_id = piece_ref[...]

    def pieces_by_lane(v):
        p0 = v.astype(BF16)
        r1 = v - p0.astype(F32)
        p1 = r1.astype(BF16)
        p2 = (r1 - p1.astype(F32)).astype(BF16)
        return jnp.where(piece_id == 0.0, p0, jnp.where(piece_id == 1.0, p1, p2))

    f_logit = proj(C_F)
    log_f = jnp.minimum(f_logit, 0.0) - jnp.log1p(jnp.exp(-jnp.abs(f_logit)))
    row = lax.broadcasted_iota(jnp.int32, (tm, tm), 0)
    col = lax.broadcasted_iota(jnp.int32, (tm, tm), 1)
    tri = (row >= col).astype(BF16)
    cum = jnp.dot(tri, pieces_by_lane(log_f), preferred_element_type=F32)
    cum = cum + pltpu.roll(cum, LANES - 1, 1) + pltpu.roll(cum, LANES - 2, 1)
    local = jnp.where(piece_id == 0.0, cum, 0.0)
    goff_ref[0, 0] = carry_ref[...] * LOG2E
    carry_ref[...] = carry_ref[...] + local[tm - 1:tm, :]

    local3 = local + pltpu.roll(local, 1, 1) + pltpu.roll(local, 2, 1)
    aug = pieces_by_lane(local3 * LOG2E).astype(F32)

    fq = proj(C_FQ) * (FOX_HEAD_DIM ** -0.5 * LOG2E)
    fk = proj(C_FK)
    feat = lane < FOX_HEAD_DIM
    aug_a = lane < AUG + N_PIECES
    aug_b = lane < AUG + 2 * N_PIECES
    for hd in range(HEADS):
        src = slice((hd // 2) * LANES, (hd // 2 + 1) * LANES)
        dst = slice(hd * HEAD_PAD, (hd + 1) * HEAD_PAD)
        q_blk, k_blk = fq[:, src], fk[:, src]
        if hd % 2:
            q_blk = pltpu.roll(q_blk, FOX_HEAD_DIM, 1)
            k_blk = pltpu.roll(k_blk, FOX_HEAD_DIM, 1)
        a_t = pltpu.roll(aug, AUG - N_PIECES * hd, 1)
        b_s = pltpu.roll(-aug, AUG + N_PIECES - N_PIECES * hd, 1)
        qf_ref[0, :, dst] = jnp.where(
            feat, q_blk, jnp.where(aug_a, a_t, jnp.where(aug_b, 1.0, 0.0))).astype(BF16)
        kf_ref[0, :, dst] = jnp.where(
            feat, k_blk, jnp.where(aug_a, 1.0, jnp.where(aug_b, b_s, 0.0))).astype(BF16)
    vft = (_dot_nt(wfvt_ref[...], h) + bfvt_ref[...]).astype(BF16)

    vfill = vfill_ref[...]
    for hd in range(HEADS):
        rows = slice(hd * HEAD_V, (hd + 1) * HEAD_V)
        vmt_ref[0, hd, 0, 0:HEAD_V, :] = vmt[rows, :]
        vmt_ref[0, hd, 0, HEAD_V:, :] = vfill
        vft_ref[0, hd, 0, 0:HEAD_V, :] = vft[rows, :]
        vft_ref[0, hd, 0, HEAD_V:, :] = vfill

    ga_ref[0] = _sigmoid(proj(C_GA)).astype(BF16)
    gb_ref[0] = _sigmoid(proj(C_GB)).astype(BF16)


def _pre_call(x, pos_row, consts):
    B, S, D = x.shape
    tm = BLK
    nblk = S // tm
    tok = lambda n: pl.BlockSpec((1, tm, n), lambda b, j: (b, j, 0))
    vt_shape = jax.ShapeDtypeStruct((B, HEADS, nblk, HEAD_PAD, tm), BF16)
    vt_spec = pl.BlockSpec((1, HEADS, 1, HEAD_PAD, tm), lambda b, j: (b, 0, j, 0, 0))
    out_shape = [
        jax.ShapeDtypeStruct((B, S, HEADS * HEAD_PAD), BF16),
        jax.ShapeDtypeStruct((B, S, HEADS * HEAD_PAD), BF16),
        vt_shape,
        jax.ShapeDtypeStruct((B, S, HEADS * HEAD_PAD), BF16),
        jax.ShapeDtypeStruct((B, S, HEADS * HEAD_PAD), BF16),
        vt_shape,
        jax.ShapeDtypeStruct((B, nblk, 1, LANES), F32),
        jax.ShapeDtypeStruct((B, S, D), BF16),
        jax.ShapeDtypeStruct((B, S, D), BF16),
    ]
    out_specs = [tok(HEADS * HEAD_PAD), tok(HEADS * HEAD_PAD), vt_spec,
                 tok(HEADS * HEAD_PAD), tok(HEADS * HEAD_PAD), vt_spec,
                 pl.BlockSpec((1, 1, 1, LANES), lambda b, j: (b, j, 0, 0)),
                 tok(D), tok(D)]
    in_specs = ([tok(D), pl.BlockSpec((1, 1, tm), lambda b, j: (b, 0, j))]
                + [_const_spec(a.shape) for a in consts])
    return pl.pallas_call(
        _pre_kernel,
        grid=(B, nblk),
        in_specs=in_specs,
        out_specs=out_specs,
        out_shape=out_shape,
        scratch_shapes=[pltpu.VMEM((1, LANES), F32)],
        compiler_params=pltpu.CompilerParams(
            dimension_semantics=("parallel", "arbitrary"), vmem_limit_bytes=VMEM_LIMIT),
        name="pre",
    )(x, pos_row, *consts)


def _attn_kernel(*refs, fox, nblk):
    if fox:
        goff_ref, q_ref, k_ref, vt_ref, o_ref, s_ref, p_ref, mx_ref, m_ref, a_ref, acc_ref = refs
    else:
        q_ref, k_ref, vt_ref, o_ref, s_ref, p_ref, mx_ref, m_ref, a_ref, acc_ref = refs
    t = BLK
    half = t // 2
    g_base = (pl.program_id(0) * N_PAIRS + pl.program_id(1)) * 2

    def stage_q(i, j, diag):
        qoff = pl.multiple_of(i * t, t)
        koff = pl.multiple_of(j * t, t)
        for hd in range(2):
            lanes = slice(hd * HEAD_PAD, (hd + 1) * HEAD_PAD)
            if not diag:
                s = _dot_nt(k_ref[0, pl.ds(koff, t), lanes], q_ref[0, pl.ds(qoff, t), lanes])
                s_ref[hd] = s
                mx_ref[hd] = jnp.max(s, axis=0, keepdims=True)
                continue
            causal = (lax.broadcasted_iota(jnp.int32, (half, half), 0)
                      <= lax.broadcasted_iota(jnp.int32, (half, half), 1))
            k_lo = k_ref[0, pl.ds(koff, half), lanes]
            k_hi = k_ref[0, pl.ds(pl.multiple_of(koff + half, half), half), lanes]
            s_lo = _dot_nt(k_lo, q_ref[0, pl.ds(qoff, t), lanes])
            s_hh = _dot_nt(k_hi, q_ref[0, pl.ds(pl.multiple_of(qoff + half, half), half), lanes])
            s_ll = jnp.where(causal, s_lo[:, :half], -jnp.inf)
            s_hh = jnp.where(causal, s_hh, -jnp.inf)
            s_ref[hd, :half, :half] = s_ll
            s_ref[hd, :half, half:] = s_lo[:, half:]
            s_ref[hd, half:, :half] = jnp.full((half, half), -jnp.inf, F32)
            s_ref[hd, half:, half:] = s_hh
            mx_ref[hd, :, :half] = jnp.max(s_ll, axis=0, keepdims=True)
            mx_ref[hd, :, half:] = jnp.maximum(jnp.max(s_lo[:, half:], axis=0, keepdims=True),
                                               jnp.max(s_hh, axis=0, keepdims=True))

    def stage_e(i, j):
        for hd in range(2):
            m_prev = m_ref[hd]
            if fox:
                delta = goff_ref[g_base + hd, i] - goff_ref[g_base + hd, j]
                m_new = jnp.maximum(m_prev, mx_ref[hd] + delta)
                shift = m_new - delta
            else:
                m_new = jnp.maximum(m_prev, mx_ref[hd])
                shift = m_new
            p_ref[hd] = jnp.exp2(s_ref[hd] - shift).astype(BF16)
            a_ref[hd] = jnp.exp2(m_prev - m_new)
            m_ref[hd] = m_new

    def stage_p(j, diag=False):
        for hd in range(2):
            if not diag:
                acc_ref[hd] = a_ref[hd] * acc_ref[hd] + jnp.dot(
                    vt_ref[0, hd, j], p_ref[hd], preferred_element_type=F32)
                continue
            acc_ref[hd, :, :half] = a_ref[hd, :, :half] * acc_ref[hd, :, :half] + jnp.dot(
                vt_ref[0, hd, j, :, :half], p_ref[hd, :half, :half], preferred_element_type=F32)
            acc_ref[hd, :, half:] = a_ref[hd, :, half:] * acc_ref[hd, :, half:] + jnp.dot(
                vt_ref[0, hd, j], p_ref[hd, :, half:], preferred_element_type=F32)

    def finish(i):
        outs = []
        for hd in range(2):
            acc = acc_ref[hd]
            outs.append(acc[0:HEAD_V, :] / acc[HEAD_V:HEAD_V + 1, :])
        o_ref[0, pl.ds(pl.multiple_of(i * t, t), t), :] = jnp.transpose(
            jnp.concatenate(outs, axis=0)).astype(o_ref.dtype)
        reset()

    def reset():
        m_ref[...] = jnp.full(m_ref.shape, -jnp.inf, F32)
        acc_ref[...] = jnp.zeros_like(acc_ref)

    reset()
    stage_q(0, 0, True)
    stage_e(0, 0)
    stage_q(1, 0, False)

    def step(prev_diag, next_diag, i, j):
        if prev_diag:
            stage_p(i - 1, diag=True)
            finish(i - 1)
        else:
            stage_p(j - 1)
        stage_e(i, j)
        cur_diag = j == i
        ni = jnp.where(cur_diag, i + 1, i)
        nj = jnp.where(cur_diag, 0, j + 1)
        stage_q(ni, nj, next_diag)

    def body(_, ij):
        i, j = ij
        prev_diag = j == 0
        next_diag = j + 1 == i

        def variant(pd, nd):
            return lambda: step(pd, nd, i, j)

        lax.cond(prev_diag,
                 lambda: lax.cond(next_diag, variant(True, True), variant(True, False)),
                 lambda: lax.cond(next_diag, variant(False, True), variant(False, False)))
        cur_diag = j == i
        return (jnp.where(cur_diag, i + 1, i), jnp.where(cur_diag, 0, j + 1))

    n_tiles = nblk * (nblk + 1) // 2
    lax.fori_loop(1, n_tiles - 1, body, (jnp.int32(1), jnp.int32(0)))
    stage_p(nblk - 2)
    stage_e(nblk - 1, nblk - 1)
    stage_p(nblk - 1, diag=True)
    finish(nblk - 1)


def _attn_call(q, k, vt, goff=None):
    B, S, _ = q.shape
    fox = goff is not None
    t = BLK
    nblk = S // t
    assert nblk >= 2
    in_specs = [
        pl.BlockSpec((1, S, 2 * HEAD_PAD), lambda b, p: (b, 0, p)),
        pl.BlockSpec((1, S, 2 * HEAD_PAD), lambda b, p: (b, 0, p)),
        pl.BlockSpec((1, 2, nblk, HEAD_PAD, t), lambda b, p: (b, p, 0, 0, 0)),
    ]
    args = [q, k, vt]
    if fox:
        in_specs = [pl.BlockSpec(memory_space=pltpu.SMEM)] + in_specs
        args = [goff] + args
    stat = pltpu.VMEM((2, 1, t), F32)
    return pl.pallas_call(
        functools.partial(_attn_kernel, fox=fox, nblk=nblk),
        grid=(B, N_PAIRS),
        in_specs=in_specs,
        out_specs=pl.BlockSpec((1, S, 2 * HEAD_V), lambda b, p: (b, 0, p)),
        out_shape=jax.ShapeDtypeStruct((B, S, HEADS * HEAD_V), BF16),
        scratch_shapes=[pltpu.VMEM((2, t, t), F32), pltpu.VMEM((2, t, t), BF16),
                        stat, stat, stat, pltpu.VMEM((2, HEAD_PAD, t), F32)],
        compiler_params=pltpu.CompilerParams(
            dimension_semantics=("parallel", "parallel"),
            vmem_limit_bytes=VMEM_LIMIT),
        name="attn_fox" if fox else "attn_mla",
    )(*args)


def _post_kernel(om_ref, of_ref, ga_ref, gb_ref, x_ref, wom_ref, wof_ref, wout_ref,
                 g1_ref, g2_ref, w1_ref, w2_ref, g3_ref, out_ref):
    y_mla = jnp.dot(om_ref[0], wom_ref[...], preferred_element_type=F32)
    y_fox = jnp.dot(of_ref[0], wof_ref[...], preferred_element_type=F32)
    merged = ga_ref[0].astype(F32) * y_mla + gb_ref[0].astype(F32) * y_fox
    mix = jnp.dot(merged.astype(BF16), wout_ref[...], preferred_element_type=F32)
    x1 = x_ref[0] + _rms(mix, g1_ref[...])
    h2 = _rms(x1, g2_ref[...]).astype(BF16)
    m = jnp.zeros_like(x1)
    for c in range(D_FF // FF_CHUNK):
        sl = slice(c * FF_CHUNK, (c + 1) * FF_CHUNK)
        u = jnp.dot(h2, w1_ref[:, sl], preferred_element_type=F32)
        a = jnp.square(jnp.maximum(u, 0.0)).astype(BF16)
        m = m + jnp.dot(a, w2_ref[sl, :], preferred_element_type=F32)
    out_ref[0] = x1 + _rms(m, g3_ref[...])


def _post_call(o_mla, o_fox, ga, gb, x, wom, wof, wout, g1, g2, w1, w2, g3):
    B, S, D = x.shape
    tm = POST_TM
    tok = lambda n: pl.BlockSpec((1, tm, n), lambda b, j: (b, j, 0))
    in_specs = [tok(512), tok(512), tok(D), tok(D), tok(D)] + [
        _const_spec(a.shape) for a in (wom, wof, wout, g1, g2, w1, w2, g3)]
    return pl.pallas_call(
        _post_kernel,
        grid=(B, S // tm),
        in_specs=in_specs,
        out_specs=tok(D),
        out_shape=jax.ShapeDtypeStruct((B, S, D), F32),
        compiler_params=pltpu.CompilerParams(
            dimension_semantics=("parallel", "parallel"), vmem_limit_bytes=VMEM_LIMIT),
        name="post",
    )(o_mla, o_fox, ga, gb, x, wom, wof, wout, g1, g2, w1, w2, g3)


def _pad_heads(w, width):
    r = w.shape[0]
    w = w.reshape(r, HEADS, width)
    w = jnp.pad(w, ((0, 0), (0, 0), (0, HEAD_PAD - width)))
    return w.reshape(r, HEADS * HEAD_PAD)


def _arrange_in_proj(w, b):
    offs = np.cumsum((0,) + IN_SPLITS)
    cq, ckv, kr, fq, fk, fv, f, ga, gb = [w[:, offs[i]:offs[i + 1]] for i in range(9)]
    bcq, bckv, bkr, bfq, bfk, bfv, bf, bga, bgb = [b[None, offs[i]:offs[i + 1]] for i in range(9)]
    kr_pad = ((0, 0), (MLA_NOPE_DIM, LANES - MLA_NOPE_DIM - MLA_ROPE_DIM))
    f_rep = lambda a: jnp.pad(jnp.repeat(a, N_PIECES, axis=1), ((0, 0), (0, LANES - PIECE_LANES)))
    w_r = jnp.concatenate([cq, ckv, jnp.pad(kr, kr_pad), fq, fk, f_rep(f), ga, gb], axis=1)
    b_r = jnp.concatenate([bcq, bckv, jnp.pad(bkr, kr_pad), bfq, bfk, f_rep(bf), bga, bgb], axis=1)
    return w_r.astype(BF16), b_r.astype(F32), fv.T.astype(BF16), bfv.T.astype(F32)


def _static_tables():
    inv = ROPE_THETA ** (-np.arange(HALF_ROPE, dtype=np.float32) / HALF_ROPE)
    freq = inv.reshape(HALF_ROPE, 1)
    piece = np.full((1, LANES), float(N_PIECES), np.float32)
    piece[0, :PIECE_LANES] = np.arange(PIECE_LANES) % N_PIECES
    vfill = np.zeros((HEAD_PAD - HEAD_V, BLK), np.float32)
    vfill[0, :] = 1.0
    return jnp.asarray(freq), jnp.asarray(piece), jnp.asarray(vfill, BF16)


def _layer(x, pos_row, ln_pre_mix, ln_post_mix, ln_pre_mlp, ln_post_mlp, w_in, b_in, q_a_norm,
           w_uq, kv_a_norm, w_uk, w_uv, w_o_mla, w_o_fox, w_out, w_ff1, w_ff2):
    B, S, _ = x.shape
    row = lambda v: v[None, :].astype(F32)
    w_r, b_r, wfvt, bfvt = _arrange_in_proj(w_in, b_in)
    freq, piece, vfill = _static_tables()
    consts = (row(ln_pre_mix), w_r, b_r, row(q_a_norm),
              _pad_heads(w_uq, MLA_NOPE_DIM + MLA_ROPE_DIM).astype(BF16), row(kv_a_norm),
              _pad_heads(w_uk, MLA_NOPE_DIM).astype(BF16), w_uv.T.astype(BF16), wfvt, bfvt,
              freq, piece, vfill)
    (q_mla, k_mla, vt_mla, q_fox, k_fox, vt_fox, goff, ga, gb) = _pre_call(x, pos_row, consts)

    goff = jnp.transpose(goff[:, :, 0, 0:PIECE_LANES:N_PIECES], (0, 2, 1))
    goff = goff.reshape(B * FOX_HEADS, S // BLK)

    o_mla = _attn_call(q_mla, k_mla, vt_mla)
    o_fox = _attn_call(q_fox, k_fox, vt_fox, goff)

    return _post_call(o_mla, o_fox, ga, gb, x, w_o_mla.astype(BF16), w_o_fox.astype(BF16),
                      w_out.astype(BF16), row(ln_post_mix), row(ln_pre_mlp),
                      w_ff1.astype(BF16), w_ff2.astype(BF16), row(ln_post_mlp))


def kernel(x, positions, ln_pre_mix, ln_post_mix, ln_pre_mlp, ln_post_mlp, w_in, b_in,
           q_a_norm, w_uq, kv_a_norm, w_uk, w_uv, w_o_mla, w_o_fox, w_out, w_ff1, w_ff2):
    pos_row = positions[:, None, :]
    for l in range(w_in.shape[0]):
        x = _layer(x, pos_row, ln_pre_mix[l], ln_post_mix[l], ln_pre_mlp[l], ln_post_mlp[l],
                   w_in[l], b_in[l], q_a_norm[l], w_uq[l], kv_a_norm[l], w_uk[l], w_uv[l],
                   w_o_mla[l], w_o_fox[l], w_out[l], w_ff1[l], w_ff2[l])
    return x
```

```python
import functools

import numpy as np
import jax
import jax.numpy as jnp
from jax import lax
from jax.experimental import pallas as pl
from jax.experimental.pallas import tpu as pltpu

F32 = jnp.float32
BF16 = jnp.bfloat16

D_MODEL = 1024
MLA_HEADS = 8
MLA_Q_LORA = 256
MLA_KV_LORA = 128
MLA_NOPE_DIM = 64
MLA_ROPE_DIM = 32
MLA_V_DIM = 64
FOX_HEADS = 8
FOX_HEAD_DIM = 64
D_FF = 4 * D_MODEL
ROPE_THETA = 10000.0
NORM_EPS = 1e-6
IN_SPLITS = (MLA_Q_LORA, MLA_KV_LORA, MLA_ROPE_DIM,
             FOX_HEADS * FOX_HEAD_DIM, FOX_HEADS * FOX_HEAD_DIM, FOX_HEADS * FOX_HEAD_DIM,
             FOX_HEADS, D_MODEL, D_MODEL)

LANES = 128
HEAD_PAD = 128
HEADS = 8
HEAD_V = 64
HALF_ROPE = MLA_ROPE_DIM // 2
N_PAIRS = HEADS // 2
AUG = 64
N_PIECES = 3
PIECE_LANES = N_PIECES * FOX_HEADS
LOG2E = float(np.log2(np.e))
VMEM_LIMIT = 52 * 1024 * 1024

BLK = 512
POST_TM = 512
FF_CHUNK = 512

_W = (MLA_Q_LORA, MLA_KV_LORA, LANES, FOX_HEADS * FOX_HEAD_DIM, FOX_HEADS * FOX_HEAD_DIM, LANES,
      D_MODEL, D_MODEL)
_OFF = tuple(int(v) for v in np.cumsum((0,) + _W))
(C_CQ, C_CKV, C_KR, C_FQ, C_FK, C_F, C_GA, C_GB) = tuple(
    (_OFF[i], _OFF[i + 1]) for i in range(8))


def _rms(x, g):
    return x * lax.rsqrt(jnp.mean(x * x, axis=-1, keepdims=True) + NORM_EPS) * g


def _sigmoid(x):
    return 0.5 * jnp.tanh(0.5 * x) + 0.5


def _const_spec(shape):
    zeros = (0,) * len(shape)
    return pl.BlockSpec(shape, lambda *_: zeros, pipeline_mode=pl.Buffered(1))


def _dot_nt(a, b):
    return lax.dot_general(a, b, (((1,), (1,)), ((), ())), preferred_element_type=F32)


def _pre_kernel(x_ref, pos_ref, g_ref, w_ref, b_ref, qn_ref, wuq_ref, kvn_ref, wuk_ref,
                wuvt_ref, wfvt_ref, bfvt_ref, freq_ref, piece_ref, vfill_ref,
                qm_ref, km_ref, vmt_ref, qf_ref, kf_ref, vft_ref, goff_ref, ga_ref, gb_ref,
                carry_ref):
    tm = x_ref.shape[1]
    lane = lax.broadcasted_iota(jnp.int32, (tm, LANES), 1)

    @pl.when(pl.program_id(1) == 0)
    def _():
        carry_ref[...] = jnp.zeros_like(carry_ref)

    h = _rms(x_ref[0], g_ref[...]).astype(BF16)

    def proj(c):
        return (jnp.dot(h, w_ref[:, c[0]:c[1]], preferred_element_type=F32)
                + b_ref[:, c[0]:c[1]])

    ang = freq_ref[...] * pos_ref[0].astype(F32)
    cos_t = jnp.cos(ang)
    sin_t = jnp.sin(ang)
    below = (MLA_NOPE_DIM, tm)
    above = (LANES - MLA_NOPE_DIM - MLA_ROPE_DIM, tm)
    cos = jnp.transpose(jnp.concatenate(
        [jnp.ones(below, F32), cos_t, cos_t, jnp.ones(above, F32)], axis=0))
    sin = jnp.transpose(jnp.concatenate(
        [jnp.zeros(below, F32), -sin_t, sin_t, jnp.zeros(above, F32)], axis=0))
    s_lo = jnp.where(lane < MLA_NOPE_DIM + HALF_ROPE, sin, 0.0)
    s_hi = jnp.where(lane >= MLA_NOPE_DIM + HALF_ROPE, sin, 0.0)

    def rope(blk):
        return (blk * cos + pltpu.roll(blk, LANES - HALF_ROPE, 1) * s_lo
                + pltpu.roll(blk, HALF_ROPE, 1) * s_hi)

    c_q = _rms(proj(C_CQ), qn_ref[...]).astype(BF16)
    q_full = jnp.dot(c_q, wuq_ref[...], preferred_element_type=F32)
    q_scale = (MLA_NOPE_DIM + MLA_ROPE_DIM) ** -0.5 * LOG2E
    c_kv = _rms(proj(C_CKV), kvn_ref[...]).astype(BF16)
    k_nope = jnp.dot(c_kv, wuk_ref[...], preferred_element_type=F32)
    k_rot = rope(proj(C_KR))
    for hd in range(HEADS):
        sl = slice(hd * HEAD_PAD, (hd + 1) * HEAD_PAD)
        qm_ref[0, :, sl] = (rope(q_full[:, sl]) * q_scale).astype(BF16)
        km_ref[0, :, sl] = (k_nope[:, sl] + k_rot).astype(BF16)
    vmt = _dot_nt(wuvt_ref[...], c_kv).astype(BF16)

    piece_id = piece_ref[...]

    def pieces_by_lane(v):
        p0 = v.astype(BF16)
        r1 = v - p0.astype(F32)
        p1 = r1.astype(BF16)
        p2 = (r1 - p1.astype(F32)).astype(BF16)
        return jnp.where(piece_id == 0.0, p0, jnp.where(piece_id == 1.0, p1, p2))

    f_logit = proj(C_F)
    log_f = jnp.minimum(f_logit, 0.0) - jnp.log1p(jnp.exp(-jnp.abs(f_logit)))
    row = lax.broadcasted_iota(jnp.int32, (tm, tm), 0)
    col = lax.broadcasted_iota(jnp.int32, (tm, tm), 1)
    tri = (row >= col).astype(BF16)
    cum = jnp.dot(tri, pieces_by_lane(log_f), preferred_element_type=F32)
    cum = cum + pltpu.roll(cum, LANES - 1, 1) + pltpu.roll(cum, LANES - 2, 1)
    local = jnp.where(piece_id == 0.0, cum, 0.0)
    goff_ref[0, 0] = carry_ref[...] * LOG2E
    carry_ref[...] = carry_ref[...] + local[tm - 1:tm, :]

    local3 = local + pltpu.roll(local, 1, 1) + pltpu.roll(local, 2, 1)
    aug = pieces_by_lane(local3 * LOG2E).astype(F32)

    fq = proj(C_FQ) * (FOX_HEAD_DIM ** -0.5 * LOG2E)
    fk = proj(C_FK)
    feat = lane < FOX_HEAD_DIM
    aug_a = lane < AUG + N_PIECES
    aug_b = lane < AUG + 2 * N_PIECES
    for hd in range(HEADS):
        src = slice((hd // 2) * LANES, (hd // 2 + 1) * LANES)
        dst = slice(hd * HEAD_PAD, (hd + 1) * HEAD_PAD)
        q_blk, k_blk = fq[:, src], fk[:, src]
        if hd % 2:
            q_blk = pltpu.roll(q_blk, FOX_HEAD_DIM, 1)
            k_blk = pltpu.roll(k_blk, FOX_HEAD_DIM, 1)
        a_t = pltpu.roll(aug, AUG - N_PIECES * hd, 1)
        b_s = pltpu.roll(-aug, AUG + N_PIECES - N_PIECES * hd, 1)
        qf_ref[0, :, dst] = jnp.where(
            feat, q_blk, jnp.where(aug_a, a_t, jnp.where(aug_b, 1.0, 0.0))).astype(BF16)
        kf_ref[0, :, dst] = jnp.where(
            feat, k_blk, jnp.where(aug_a, 1.0, jnp.where(aug_b, b_s, 0.0))).astype(BF16)
    vft = (_dot_nt(wfvt_ref[...], h) + bfvt_ref[...]).astype(BF16)

    vfill = vfill_ref[...]
    for hd in range(HEADS):
        rows = slice(hd * HEAD_V, (hd + 1) * HEAD_V)
        vmt_ref[0, hd, 0, 0:HEAD_V, :] = vmt[rows, :]
        vmt_ref[0, hd, 0, HEAD_V:, :] = vfill
        vft_ref[0, hd, 0, 0:HEAD_V, :] = vft[rows, :]
        vft_ref[0, hd, 0, HEAD_V:, :] = vfill

    ga_ref[0] = _sigmoid(proj(C_GA)).astype(BF16)
    gb_ref[0] = _sigmoid(proj(C_GB)).astype(BF16)


def _pre_call(x, pos_row, consts):
    B, S, D = x.shape
    tm = BLK
    nblk = S // tm
    tok = lambda n: pl.BlockSpec((1, tm, n), lambda b, j: (b, j, 0))
    vt_shape = jax.ShapeDtypeStruct((B, HEADS, nblk, HEAD_PAD, tm), BF16)
    vt_spec = pl.BlockSpec((1, HEADS, 1, HEAD_PAD, tm), lambda b, j: (b, 0, j, 0, 0))
    out_shape = [
        jax.ShapeDtypeStruct((B, S, HEADS * HEAD_PAD), BF16),
        jax.ShapeDtypeStruct((B, S, HEADS * HEAD_PAD), BF16),
        vt_shape,
        jax.ShapeDtypeStruct((B, S, HEADS * HEAD_PAD), BF16),
        jax.ShapeDtypeStruct((B, S, HEADS * HEAD_PAD), BF16),
        vt_shape,
        jax.ShapeDtypeStruct((B, nblk, 1, LANES), F32),
        jax.ShapeDtypeStruct((B, S, D), BF16),
        jax.ShapeDtypeStruct((B, S, D), BF16),
    ]
    out_specs = [tok(HEADS * HEAD_PAD), tok(HEADS * HEAD_PAD), vt_spec,
                 tok(HEADS * HEAD_PAD), tok(HEADS * HEAD_PAD), vt_spec,
                 pl.BlockSpec((1, 1, 1, LANES), lambda b, j: (b, j, 0, 0)),
                 tok(D), tok(D)]
    in_specs = ([tok(D), pl.BlockSpec((1, 1, tm), lambda b, j: (b, 0, j))]
                + [_const_spec(a.shape) for a in consts])
    return pl.pallas_call(
        _pre_kernel,
        grid=(B, nblk),
        in_specs=in_specs,
        out_specs=out_specs,
        out_shape=out_shape,
        scratch_shapes=[pltpu.VMEM((1, LANES), F32)],
        compiler_params=pltpu.CompilerParams(
            dimension_semantics=("parallel", "arbitrary"), vmem_limit_bytes=VMEM_LIMIT),
        name="pre",
    )(x, pos_row, *consts)


def _attn_kernel(*refs, fox, nblk):
    if fox:
        goff_ref, q_ref, k_ref, vt_ref, o_ref, s_ref, p_ref, mx_ref, m_ref, a_ref, acc_ref = refs
    else:
        q_ref, k_ref, vt_ref, o_ref, s_ref, p_ref, mx_ref, m_ref, a_ref, acc_ref = refs
    t = BLK
    half = t // 2
    g_base = (pl.program_id(0) * N_PAIRS + pl.program_id(1)) * 2

    def stage_q(i, j, diag):
        qoff = pl.multiple_of(i * t, t)
        koff = pl.multiple_of(j * t, t)
        for hd in range(2):
            lanes = slice(hd * HEAD_PAD, (hd + 1) * HEAD_PAD)
            if not diag:
                s = _dot_nt(k_ref[0, pl.ds(koff, t), lanes], q_ref[0, pl.ds(qoff, t), lanes])
                s_ref[hd] = s
                mx_ref[hd] = jnp.max(s, axis=0, keepdims=True)
                continue
            causal = (lax.broadcasted_iota(jnp.int32, (half, half), 0)
                      <= lax.broadcasted_iota(jnp.int32, (half, half), 1))
            k_lo = k_ref[0, pl.ds(koff, half), lanes]
            k_hi = k_ref[0, pl.ds(pl.multiple_of(koff + half, half), half), lanes]
            s_lo = _dot_nt(k_lo, q_ref[0, pl.ds(qoff, t), lanes])
            s_hh = _dot_nt(k_hi, q_ref[0, pl.ds(pl.multiple_of(qoff + half, half), half), lanes])
            s_ll = jnp.where(causal, s_lo[:, :half], -jnp.inf)
            s_hh = jnp.where(causal, s_hh, -jnp.inf)
            s_ref[hd, :half, :half] = s_ll
            s_ref[hd, :half, half:] = s_lo[:, half:]
            s_ref[hd, half:, :half] = jnp.full((half, half), -jnp.inf, F32)
            s_ref[hd, half:, half:] = s_hh
            mx_ref[hd, :, :half] = jnp.max(s_ll, axis=0, keepdims=True)
            mx_ref[hd, :, half:] = jnp.maximum(jnp.max(s_lo[:, half:], axis=0, keepdims=True),
                                               jnp.max(s_hh, axis=0, keepdims=True))

    def stage_e(i, j):
        for hd in range(2):
            m_prev = m_ref[hd]
            if fox:
                delta = goff_ref[g_base + hd, i] - goff_ref[g_base + hd, j]
                m_new = jnp.maximum(m_prev, mx_ref[hd] + delta)
                shift = m_new - delta
            else:
                m_new = jnp.maximum(m_prev, mx_ref[hd])
                shift = m_new
            p_ref[hd] = jnp.exp2(s_ref[hd] - shift).astype(BF16)
            a_ref[hd] = jnp.exp2(m_prev - m_new)
            m_ref[hd] = m_new

    def stage_p(j, diag=False):
        for hd in range(2):
            if not diag:
                acc_ref[hd] = a_ref[hd] * acc_ref[hd] + jnp.dot(
                    vt_ref[0, hd, j], p_ref[hd], preferred_element_type=F32)
                continue
            acc_ref[hd, :, :half] = a_ref[hd, :, :half] * acc_ref[hd, :, :half] + jnp.dot(
                vt_ref[0, hd, j, :, :half], p_ref[hd, :half, :half], preferred_element_type=F32)
            acc_ref[hd, :, half:] = a_ref[hd, :, half:] * acc_ref[hd, :, half:] + jnp.dot(
                vt_ref[0, hd, j], p_ref[hd, :, half:], preferred_element_type=F32)

    def finish(i):
        outs = []
        for hd in range(2):
            acc = acc_ref[hd]
            outs.append(acc[0:HEAD_V, :] / acc[HEAD_V:HEAD_V + 1, :])
        o_ref[0, pl.ds(pl.multiple_of(i * t, t), t), :] = jnp.transpose(
            jnp.concatenate(outs, axis=0)).astype(o_ref.dtype)
        reset()

    def reset():
        m_ref[...] = jnp.full(m_ref.shape, -jnp.inf, F32)
        acc_ref[...] = jnp.zeros_like(acc_ref)

    def step(i, j, nxt, next_diag=False):
        if isinstance(j, int) and j == 0:
            stage_p(i - 1, diag=True)
            finish(i - 1)
        else:
            stage_p(j - 1)
        stage_e(i, j)
        stage_q(nxt[0], nxt[1], next_diag)

    def plain_step(i, j):
        step(i, j, (i, j + 1))

    reset()
    stage_q(0, 0, True)
    stage_e(0, 0)
    stage_q(1, 0, False)
    step(1, 0, (1, 1), next_diag=True)
    step(1, 1, (2, 0))

    def row(i, carry):
        step(i, 0, (i, 1))
        n_plain = i - 2

        def pair(kk, c):
            plain_step(i, 1 + 2 * kk)
            plain_step(i, 2 + 2 * kk)
            return c

        lax.fori_loop(0, lax.shift_right_logical(n_plain, 1), pair, 0)

        @pl.when(lax.bitwise_and(n_plain, 1) == 1)
        def _():
            plain_step(i, i - 2)

        step(i, i - 1, (i, i), next_diag=True)
        step(i, i, (jnp.minimum(i + 1, nblk - 1), 0))
        return carry

    lax.fori_loop(2, nblk, row, 0)
    stage_p(nblk - 1, diag=True)
    finish(nblk - 1)


def _attn_call(q, k, vt, goff=None):
    B, S, _ = q.shape
    fox = goff is not None
    t = BLK
    nblk = S // t
    assert nblk >= 2
    in_specs = [
        pl.BlockSpec((1, S, 2 * HEAD_PAD), lambda b, p: (b, 0, p)),
        pl.BlockSpec((1, S, 2 * HEAD_PAD), lambda b, p: (b, 0, p)),
        pl.BlockSpec((1, 2, nblk, HEAD_PAD, t), lambda b, p: (b, p, 0, 0, 0)),
    ]
    args = [q, k, vt]
    if fox:
        in_specs = [pl.BlockSpec(memory_space=pltpu.SMEM)] + in_specs
        args = [goff] + args
    stat = pltpu.VMEM((2, 1, t), F32)
    return pl.pallas_call(
        functools.partial(_attn_kernel, fox=fox, nblk=nblk),
        grid=(B, N_PAIRS),
        in_specs=in_specs,
        out_specs=pl.BlockSpec((1, S, 2 * HEAD_V), lambda b, p: (b, 0, p)),
        out_shape=jax.ShapeDtypeStruct((B, S, HEADS * HEAD_V), BF16),
        scratch_shapes=[pltpu.VMEM((2, t, t), F32), pltpu.VMEM((2, t, t), BF16),
                        stat, stat, stat, pltpu.VMEM((2, HEAD_PAD, t), F32)],
        compiler_params=pltpu.CompilerParams(
            dimension_semantics=("parallel", "parallel"),
            vmem_limit_bytes=VMEM_LIMIT),
        name="attn_fox" if fox else "attn_mla",
    )(*args)


def _post_kernel(om_ref, of_ref, ga_ref, gb_ref, x_ref, wom_ref, wof_ref, wout_ref,
                 g1_ref, g2_ref, w1_ref, w2_ref, g3_ref, out_ref):
    y_mla = jnp.dot(om_ref[0], wom_ref[...], preferred_element_type=F32)
    y_fox = jnp.dot(of_ref[0], wof_ref[...], preferred_element_type=F32)
    merged = ga_ref[0].astype(F32) * y_mla + gb_ref[0].astype(F32) * y_fox
    mix = jnp.dot(merged.astype(BF16), wout_ref[...], preferred_element_type=F32)
    x1 = x_ref[0] + _rms(mix, g1_ref[...])
    h2 = _rms(x1, g2_ref[...]).astype(BF16)
    m = jnp.zeros_like(x1)
    for c in range(D_FF // FF_CHUNK):
        sl = slice(c * FF_CHUNK, (c + 1) * FF_CHUNK)
        u = jnp.dot(h2, w1_ref[:, sl], preferred_element_type=F32)
        a = jnp.square(jnp.maximum(u, 0.0)).astype(BF16)
        m = m + jnp.dot(a, w2_ref[sl, :], preferred_element_type=F32)
    out_ref[0] = x1 + _rms(m, g3_ref[...])


def _post_call(o_mla, o_fox, ga, gb, x, wom, wof, wout, g1, g2, w1, w2, g3):
    B, S, D = x.shape
    tm = POST_TM
    tok = lambda n: pl.BlockSpec((1, tm, n), lambda b, j: (b, j, 0))
    in_specs = [tok(512), tok(512), tok(D), tok(D), tok(D)] + [
        _const_spec(a.shape) for a in (wom, wof, wout, g1, g2, w1, w2, g3)]
    return pl.pallas_call(
        _post_kernel,
        grid=(B, S // tm),
        in_specs=in_specs,
        out_specs=tok(D),
        out_shape=jax.ShapeDtypeStruct((B, S, D), F32),
        compiler_params=pltpu.CompilerParams(
            dimension_semantics=("parallel", "parallel"), vmem_limit_bytes=VMEM_LIMIT),
        name="post",
    )(o_mla, o_fox, ga, gb, x, wom, wof, wout, g1, g2, w1, w2, g3)


def _pad_heads(w, width):
    r = w.shape[0]
    w = w.reshape(r, HEADS, width)
    w = jnp.pad(w, ((0, 0), (0, 0), (0, HEAD_PAD - width)))
    return w.reshape(r, HEADS * HEAD_PAD)


def _arrange_in_proj(w, b):
    offs = np.cumsum((0,) + IN_SPLITS)
    cq, ckv, kr, fq, fk, fv, f, ga, gb = [w[:, offs[i]:offs[i + 1]] for i in range(9)]
    bcq, bckv, bkr, bfq, bfk, bfv, bf, bga, bgb = [b[None, offs[i]:offs[i + 1]] for i in range(9)]
    kr_pad = ((0, 0), (MLA_NOPE_DIM, LANES - MLA_NOPE_DIM - MLA_ROPE_DIM))
    f_rep = lambda a: jnp.pad(jnp.repeat(a, N_PIECES, axis=1), ((0, 0), (0, LANES - PIECE_LANES)))
    w_r = jnp.concatenate([cq, ckv, jnp.pad(kr, kr_pad), fq, fk, f_rep(f), ga, gb], axis=1)
    b_r = jnp.concatenate([bcq, bckv, jnp.pad(bkr, kr_pad), bfq, bfk, f_rep(bf), bga, bgb], axis=1)
    return w_r.astype(BF16), b_r.astype(F32), fv.T.astype(BF16), bfv.T.astype(F32)


def _static_tables():
    inv = ROPE_THETA ** (-np.arange(HALF_ROPE, dtype=np.float32) / HALF_ROPE)
    freq = inv.reshape(HALF_ROPE, 1)
    piece = np.full((1, LANES), float(N_PIECES), np.float32)
    piece[0, :PIECE_LANES] = np.arange(PIECE_LANES) % N_PIECES
    vfill = np.zeros((HEAD_PAD - HEAD_V, BLK), np.float32)
    vfill[0, :] = 1.0
    return jnp.asarray(freq), jnp.asarray(piece), jnp.asarray(vfill, BF16)


def _layer(x, pos_row, ln_pre_mix, ln_post_mix, ln_pre_mlp, ln_post_mlp, w_in, b_in, q_a_norm,
           w_uq, kv_a_norm, w_uk, w_uv, w_o_mla, w_o_fox, w_out, w_ff1, w_ff2):
    B, S, _ = x.shape
    row = lambda v: v[None, :].astype(F32)
    w_r, b_r, wfvt, bfvt = _arrange_in_proj(w_in, b_in)
    freq, piece, vfill = _static_tables()
    consts = (row(ln_pre_mix), w_r, b_r, row(q_a_norm),
              _pad_heads(w_uq, MLA_NOPE_DIM + MLA_ROPE_DIM).astype(BF16), row(kv_a_norm),
              _pad_heads(w_uk, MLA_NOPE_DIM).astype(BF16), w_uv.T.astype(BF16), wfvt, bfvt,
              freq, piece, vfill)
    (q_mla, k_mla, vt_mla, q_fox, k_fox, vt_fox, goff, ga, gb) = _pre_call(x, pos_row, consts)

    goff = jnp.transpose(goff[:, :, 0, 0:PIECE_LANES:N_PIECES], (0, 2, 1))
    goff = goff.reshape(B * FOX_HEADS, S // BLK)

    o_mla = _attn_call(q_mla, k_mla, vt_mla)
    o_fox = _attn_call(q_fox, k_fox, vt_fox, goff)

    return _post_call(o_mla, o_fox, ga, gb, x, w_o_mla.astype(BF16), w_o_fox.astype(BF16),
                      w_out.astype(BF16), row(ln_post_mix), row(ln_pre_mlp),
                      w_ff1.astype(BF16), w_ff2.astype(BF16), row(ln_post_mlp))


def kernel(x, positions, ln_pre_mix, ln_post_mix, ln_pre_mlp, ln_post_mlp, w_in, b_in,
           q_a_norm, w_uq, kv_a_norm, w_uk, w_uv, w_o_mla, w_o_fox, w_out, w_ff1, w_ff2):
    pos_row = positions[:, None, :]
    for l in range(w_in.shape[0]):
        x = _layer(x, pos_row, ln_pre_mix[l], ln_post_mix[l], ln_pre_mlp[l], ln_post_mlp[l],
                   w_in[l], b_in[l], q_a_norm[l], w_uq[l], kv_a_norm[l], w_uk[l], w_uv[l],
                   w_o_mla[l], w_o_fox[l], w_out[l], w_ff1[l], w_ff2[l])
    return x
```

```python
import functools

import numpy as np
import jax
import jax.numpy as jnp
from jax import lax
from jax.experimental import pallas as pl
from jax.experimental.pallas import tpu as pltpu

F32 = jnp.float32
BF16 = jnp.bfloat16

D_MODEL = 1024
MLA_HEADS = 8
MLA_Q_LORA = 256
MLA_KV_LORA = 128
MLA_NOPE_DIM = 64
MLA_ROPE_DIM = 32
MLA_V_DIM = 64
FOX_HEADS = 8
FOX_HEAD_DIM = 64
D_FF = 4 * D_MODEL
ROPE_THETA = 10000.0
NORM_EPS = 1e-6
IN_SPLITS = (MLA_Q_LORA, MLA_KV_LORA, MLA_ROPE_DIM,
             FOX_HEADS * FOX_HEAD_DIM, FOX_HEADS * FOX_HEAD_DIM, FOX_HEADS * FOX_HEAD_DIM,
             FOX_HEADS, D_MODEL, D_MODEL)

LANES = 128
HEAD_PAD = 128
HEADS = 8
HEAD_V = 64
VT_ROWS = HEAD_V + 16
HALF_ROPE = MLA_ROPE_DIM // 2
N_PAIRS = HEADS // 2
AUG = 64
N_PIECES = 3
PIECE_LANES = N_PIECES * FOX_HEADS
LOG2E = float(np.log2(np.e))
VMEM_LIMIT = 52 * 1024 * 1024

BLK = 512
POST_TM = 512
FF_CHUNK = 512

_W = (MLA_Q_LORA, MLA_KV_LORA, LANES, FOX_HEADS * FOX_HEAD_DIM, FOX_HEADS * FOX_HEAD_DIM, LANES,
      D_MODEL, D_MODEL)
_OFF = tuple(int(v) for v in np.cumsum((0,) + _W))
(C_CQ, C_CKV, C_KR, C_FQ, C_FK, C_F, C_GA, C_GB) = tuple(
    (_OFF[i], _OFF[i + 1]) for i in range(8))


def _rms(x, g):
    return x * lax.rsqrt(jnp.mean(x * x, axis=-1, keepdims=True) + NORM_EPS) * g


def _sigmoid(x):
    return 0.5 * jnp.tanh(0.5 * x) + 0.5


def _const_spec(shape):
    zeros = (0,) * len(shape)
    return pl.BlockSpec(shape, lambda *_: zeros, pipeline_mode=pl.Buffered(1))


def _dot_nt(a, b):
    return lax.dot_general(a, b, (((1,), (1,)), ((), ())), preferred_element_type=F32)


def _pre_kernel(x_ref, pos_ref, g_ref, w_ref, b_ref, qn_ref, wuq_ref, kvn_ref, wuk_ref,
                wuvt_ref, wfvt_ref, bfvt_ref, freq_ref, piece_ref, vfill_ref,
                qm_ref, km_ref, vmt_ref, qf_ref, kf_ref, vft_ref, goff_ref, ga_ref, gb_ref,
                carry_ref):
    tm = x_ref.shape[1]
    lane = lax.broadcasted_iota(jnp.int32, (tm, LANES), 1)

    @pl.when(pl.program_id(1) == 0)
    def _():
        carry_ref[...] = jnp.zeros_like(carry_ref)

    h = _rms(x_ref[0], g_ref[...]).astype(BF16)

    def proj(c):
        return (jnp.dot(h, w_ref[:, c[0]:c[1]], preferred_element_type=F32)
                + b_ref[:, c[0]:c[1]])

    ang = freq_ref[...] * pos_ref[0].astype(F32)
    cos_t = jnp.cos(ang)
    sin_t = jnp.sin(ang)
    below = (MLA_NOPE_DIM, tm)
    above = (LANES - MLA_NOPE_DIM - MLA_ROPE_DIM, tm)
    cos = jnp.transpose(jnp.concatenate(
        [jnp.ones(below, F32), cos_t, cos_t, jnp.ones(above, F32)], axis=0))
    sin = jnp.transpose(jnp.concatenate(
        [jnp.zeros(below, F32), -sin_t, sin_t, jnp.zeros(above, F32)], axis=0))
    s_lo = jnp.where(lane < MLA_NOPE_DIM + HALF_ROPE, sin, 0.0)
    s_hi = jnp.where(lane >= MLA_NOPE_DIM + HALF_ROPE, sin, 0.0)

    def rope(blk):
        return (blk * cos + pltpu.roll(blk, LANES - HALF_ROPE, 1) * s_lo
                + pltpu.roll(blk, HALF_ROPE, 1) * s_hi)

    c_q = _rms(proj(C_CQ), qn_ref[...]).astype(BF16)
    q_full = jnp.dot(c_q, wuq_ref[...], preferred_element_type=F32)
    q_scale = (MLA_NOPE_DIM + MLA_ROPE_DIM) ** -0.5 * LOG2E
    c_kv = _rms(proj(C_CKV), kvn_ref[...]).astype(BF16)
    k_nope = jnp.dot(c_kv, wuk_ref[...], preferred_element_type=F32)
    k_rot = rope(proj(C_KR))
    for hd in range(HEADS):
        sl = slice(hd * HEAD_PAD, (hd + 1) * HEAD_PAD)
        qm_ref[0, :, sl] = (rope(q_full[:, sl]) * q_scale).astype(BF16)
        km_ref[0, :, sl] = (k_nope[:, sl] + k_rot).astype(BF16)
    vmt = _dot_nt(wuvt_ref[...], c_kv).astype(BF16)

    piece_id = piece_ref[...]

    def pieces_by_lane(v):
        p0 = v.astype(BF16)
        r1 = v - p0.astype(F32)
        p1 = r1.astype(BF16)
        p2 = (r1 - p1.astype(F32)).astype(BF16)
        return jnp.where(piece_id == 0.0, p0, jnp.where(piece_id == 1.0, p1, p2))

    f_logit = proj(C_F)
    log_f = jnp.minimum(f_logit, 0.0) - jnp.log1p(jnp.exp(-jnp.abs(f_logit)))
    row = lax.broadcasted_iota(jnp.int32, (tm, tm), 0)
    col = lax.broadcasted_iota(jnp.int32, (tm, tm), 1)
    tri = (row >= col).astype(BF16)
    cum = jnp.dot(tri, pieces_by_lane(log_f), preferred_element_type=F32)
    cum = cum + pltpu.roll(cum, LANES - 1, 1) + pltpu.roll(cum, LANES - 2, 1)
    local = jnp.where(piece_id == 0.0, cum, 0.0)
    goff_ref[0, 0] = carry_ref[...] * LOG2E
    carry_ref[...] = carry_ref[...] + local[tm - 1:tm, :]

    local3 = local + pltpu.roll(local, 1, 1) + pltpu.roll(local, 2, 1)
    aug = pieces_by_lane(local3 * LOG2E).astype(F32)

    fq = proj(C_FQ) * (FOX_HEAD_DIM ** -0.5 * LOG2E)
    fk = proj(C_FK)
    feat = lane < FOX_HEAD_DIM
    aug_a = lane < AUG + N_PIECES
    aug_b = lane < AUG + 2 * N_PIECES
    for hd in range(HEADS):
        src = slice((hd // 2) * LANES, (hd // 2 + 1) * LANES)
        dst = slice(hd * HEAD_PAD, (hd + 1) * HEAD_PAD)
        q_blk, k_blk = fq[:, src], fk[:, src]
        if hd % 2:
            q_blk = pltpu.roll(q_blk, FOX_HEAD_DIM, 1)
            k_blk = pltpu.roll(k_blk, FOX_HEAD_DIM, 1)
        a_t = pltpu.roll(aug, AUG - N_PIECES * hd, 1)
        b_s = pltpu.roll(-aug, AUG + N_PIECES - N_PIECES * hd, 1)
        qf_ref[0, :, dst] = jnp.where(
            feat, q_blk, jnp.where(aug_a, a_t, jnp.where(aug_b, 1.0, 0.0))).astype(BF16)
        kf_ref[0, :, dst] = jnp.where(
            feat, k_blk, jnp.where(aug_a, 1.0, jnp.where(aug_b, b_s, 0.0))).astype(BF16)
    vft = (_dot_nt(wfvt_ref[...], h) + bfvt_ref[...]).astype(BF16)

    vfill = vfill_ref[...]
    for hd in range(HEADS):
        rows = slice(hd * HEAD_V, (hd + 1) * HEAD_V)
        vmt_ref[0, hd, 0, 0:HEAD_V, :] = vmt[rows, :]
        vmt_ref[0, hd, 0, HEAD_V:, :] = vfill
        vft_ref[0, hd, 0, 0:HEAD_V, :] = vft[rows, :]
        vft_ref[0, hd, 0, HEAD_V:, :] = vfill

    ga_ref[0] = _sigmoid(proj(C_GA)).astype(BF16)
    gb_ref[0] = _sigmoid(proj(C_GB)).astype(BF16)


def _pre_call(x, pos_row, consts):
    B, S, D = x.shape
    tm = BLK
    nblk = S // tm
    tok = lambda n: pl.BlockSpec((1, tm, n), lambda b, j: (b, j, 0))
    vt_shape = jax.ShapeDtypeStruct((B, HEADS, nblk, VT_ROWS, tm), BF16)
    vt_spec = pl.BlockSpec((1, HEADS, 1, VT_ROWS, tm), lambda b, j: (b, 0, j, 0, 0))
    out_shape = [
        jax.ShapeDtypeStruct((B, S, HEADS * HEAD_PAD), BF16),
        jax.ShapeDtypeStruct((B, S, HEADS * HEAD_PAD), BF16),
        vt_shape,
        jax.ShapeDtypeStruct((B, S, HEADS * HEAD_PAD), BF16),
        jax.ShapeDtypeStruct((B, S, HEADS * HEAD_PAD), BF16),
        vt_shape,
        jax.ShapeDtypeStruct((B, nblk, 1, LANES), F32),
        jax.ShapeDtypeStruct((B, S, D), BF16),
        jax.ShapeDtypeStruct((B, S, D), BF16),
    ]
    out_specs = [tok(HEADS * HEAD_PAD), tok(HEADS * HEAD_PAD), vt_spec,
                 tok(HEADS * HEAD_PAD), tok(HEADS * HEAD_PAD), vt_spec,
                 pl.BlockSpec((1, 1, 1, LANES), lambda b, j: (b, j, 0, 0)),
                 tok(D), tok(D)]
    in_specs = ([tok(D), pl.BlockSpec((1, 1, tm), lambda b, j: (b, 0, j))]
                + [_const_spec(a.shape) for a in consts])
    return pl.pallas_call(
        _pre_kernel,
        grid=(B, nblk),
        in_specs=in_specs,
        out_specs=out_specs,
        out_shape=out_shape,
        scratch_shapes=[pltpu.VMEM((1, LANES), F32)],
        compiler_params=pltpu.CompilerParams(
            dimension_semantics=("parallel", "arbitrary"), vmem_limit_bytes=VMEM_LIMIT),
        name="pre",
    )(x, pos_row, *consts)


def _attn_kernel(*refs, fox, nblk):
    if fox:
        goff_ref, q_ref, k_ref, vt_ref, o_ref, s_ref, p_ref, mx_ref, m_ref, a_ref, acc_ref = refs
    else:
        q_ref, k_ref, vt_ref, o_ref, s_ref, p_ref, mx_ref, m_ref, a_ref, acc_ref = refs
    t = BLK
    half = t // 2
    g_base = (pl.program_id(0) * N_PAIRS + pl.program_id(1)) * 2

    def stage_q(i, j, diag):
        qoff = pl.multiple_of(i * t, t)
        koff = pl.multiple_of(j * t, t)
        for hd in range(2):
            lanes = slice(hd * HEAD_PAD, (hd + 1) * HEAD_PAD)
            if not diag:
                s = _dot_nt(k_ref[0, pl.ds(koff, t), lanes], q_ref[0, pl.ds(qoff, t), lanes])
                s_ref[hd] = s
                mx_ref[hd] = jnp.max(s, axis=0, keepdims=True)
                continue
            causal = (lax.broadcasted_iota(jnp.int32, (half, half), 0)
                      <= lax.broadcasted_iota(jnp.int32, (half, half), 1))
            k_lo = k_ref[0, pl.ds(koff, half), lanes]
            k_hi = k_ref[0, pl.ds(pl.multiple_of(koff + half, half), half), lanes]
            s_lo = _dot_nt(k_lo, q_ref[0, pl.ds(qoff, t), lanes])
            s_hh = _dot_nt(k_hi, q_ref[0, pl.ds(pl.multiple_of(qoff + half, half), half), lanes])
            s_ll = jnp.where(causal, s_lo[:, :half], -jnp.inf)
            s_hh = jnp.where(causal, s_hh, -jnp.inf)
            s_ref[hd, :half, :half] = s_ll
            s_ref[hd, :half, half:] = s_lo[:, half:]
            s_ref[hd, half:, :half] = jnp.full((half, half), -jnp.inf, F32)
            s_ref[hd, half:, half:] = s_hh
            mx_ref[hd, :, :half] = jnp.max(s_ll, axis=0, keepdims=True)
            mx_ref[hd, :, half:] = jnp.maximum(jnp.max(s_lo[:, half:], axis=0, keepdims=True),
                                               jnp.max(s_hh, axis=0, keepdims=True))

    def stage_e(i, j):
        for hd in range(2):
            m_prev = m_ref[hd]
            if fox:
                delta = goff_ref[g_base + hd, i] - goff_ref[g_base + hd, j]
                m_new = jnp.maximum(m_prev, mx_ref[hd] + delta)
                shift = m_new - delta
            else:
                m_new = jnp.maximum(m_prev, mx_ref[hd])
                shift = m_new
            p_ref[hd] = jnp.exp2(s_ref[hd] - shift).astype(BF16)
            a_ref[hd] = jnp.exp2(m_prev - m_new)
            m_ref[hd] = m_new

    def stage_p(j, diag=False):
        for hd in range(2):
            if not diag:
                acc_ref[hd] = a_ref[hd] * acc_ref[hd] + jnp.dot(
                    vt_ref[0, hd, j], p_ref[hd], preferred_element_type=F32)
                continue
            acc_ref[hd, :, :half] = a_ref[hd, :, :half] * acc_ref[hd, :, :half] + jnp.dot(
                vt_ref[0, hd, j, :, :half], p_ref[hd, :half, :half], preferred_element_type=F32)
            acc_ref[hd, :, half:] = a_ref[hd, :, half:] * acc_ref[hd, :, half:] + jnp.dot(
                vt_ref[0, hd, j], p_ref[hd, :, half:], preferred_element_type=F32)

    def finish(i):
        outs = []
        for hd in range(2):
            acc = acc_ref[hd]
            outs.append(acc[0:HEAD_V, :] / acc[HEAD_V:HEAD_V + 1, :])
        o_ref[0, pl.ds(pl.multiple_of(i * t, t), t), :] = jnp.transpose(
            jnp.concatenate(outs, axis=0)).astype(o_ref.dtype)
        reset()

    def reset():
        m_ref[...] = jnp.full(m_ref.shape, -jnp.inf, F32)
        acc_ref[...] = jnp.zeros_like(acc_ref)

    def step(i, j, nxt, next_diag=False):
        if isinstance(j, int) and j == 0:
            stage_p(i - 1, diag=True)
            finish(i - 1)
        else:
            stage_p(j - 1)
        stage_e(i, j)
        if nxt is not None:
            stage_q(nxt[0], nxt[1], next_diag)

    def plain_steps(i, j0, count):
        for u in range(count):
            step(i, j0 + u, (i, j0 + u + 1))

    def plain_run(i, n_plain):
        def pair(_, j):
            plain_steps(i, j, 2)
            return j + 2

        j = lax.fori_loop(0, lax.shift_right_logical(n_plain, 1), pair, jnp.int32(1))

        @pl.when(lax.bitwise_and(n_plain, 1) != 0)
        def _():
            plain_steps(i, j, 1)

    last = nblk - 1
    reset()
    stage_q(0, 0, True)
    stage_e(0, 0)
    stage_q(1, 0, False)
    step(1, 0, (1, 1), next_diag=True)
    step(1, 1, (2, 0))
    step(2, 0, (2, 1))

    def row(i, carry):
        plain_run(i, i - 2)
        step(i, i - 1, (i, i), next_diag=True)
        step(i, i, (i + 1, 0))
        step(i + 1, 0, (i + 1, 1))
        return carry

    lax.fori_loop(2, last, row, 0)
    plain_run(last, jnp.int32(last - 2))
    step(last, last - 1, (last, last), next_diag=True)
    step(last, last, None)
    stage_p(last, diag=True)
    finish(last)


def _attn_call(q, k, vt, goff=None):
    B, S, _ = q.shape
    fox = goff is not None
    t = BLK
    nblk = S // t
    assert nblk >= 2
    in_specs = [
        pl.BlockSpec((1, S, 2 * HEAD_PAD), lambda b, p: (b, 0, p)),
        pl.BlockSpec((1, S, 2 * HEAD_PAD), lambda b, p: (b, 0, p)),
        pl.BlockSpec((1, 2, nblk, VT_ROWS, t), lambda b, p: (b, p, 0, 0, 0)),
    ]
    args = [q, k, vt]
    if fox:
        in_specs = [pl.BlockSpec(memory_space=pltpu.SMEM)] + in_specs
        args = [goff] + args
    stat = pltpu.VMEM((2, 1, t), F32)
    return pl.pallas_call(
        functools.partial(_attn_kernel, fox=fox, nblk=nblk),
        grid=(B, N_PAIRS),
        in_specs=in_specs,
        out_specs=pl.BlockSpec((1, S, 2 * HEAD_V), lambda b, p: (b, 0, p)),
        out_shape=jax.ShapeDtypeStruct((B, S, HEADS * HEAD_V), BF16),
        scratch_shapes=[pltpu.VMEM((2, t, t), F32), pltpu.VMEM((2, t, t), BF16),
                        stat, stat, stat, pltpu.VMEM((2, VT_ROWS, t), F32)],
        compiler_params=pltpu.CompilerParams(
            dimension_semantics=("parallel", "parallel"),
            vmem_limit_bytes=VMEM_LIMIT),
        name="attn_fox" if fox else "attn_mla",
    )(*args)


def _post_kernel(om_ref, of_ref, ga_ref, gb_ref, x_ref, wom_ref, wof_ref, wout_ref,
                 g1_ref, g2_ref, w1_ref, w2_ref, g3_ref, out_ref):
    y_mla = jnp.dot(om_ref[0], wom_ref[...], preferred_element_type=F32)
    y_fox = jnp.dot(of_ref[0], wof_ref[...], preferred_element_type=F32)
    merged = ga_ref[0].astype(F32) * y_mla + gb_ref[0].astype(F32) * y_fox
    mix = jnp.dot(merged.astype(BF16), wout_ref[...], preferred_element_type=F32)
    x1 = x_ref[0] + _rms(mix, g1_ref[...])
    h2 = _rms(x1, g2_ref[...]).astype(BF16)
    m = jnp.zeros_like(x1)
    for c in range(D_FF // FF_CHUNK):
        sl = slice(c * FF_CHUNK, (c + 1) * FF_CHUNK)
        u = jnp.dot(h2, w1_ref[:, sl], preferred_element_type=F32)
        a = jnp.square(jnp.maximum(u, 0.0)).astype(BF16)
        m = m + jnp.dot(a, w2_ref[sl, :], preferred_element_type=F32)
    out_ref[0] = x1 + _rms(m, g3_ref[...])


def _post_call(o_mla, o_fox, ga, gb, x, wom, wof, wout, g1, g2, w1, w2, g3):
    B, S, D = x.shape
    tm = POST_TM
    tok = lambda n: pl.BlockSpec((1, tm, n), lambda b, j: (b, j, 0))
    in_specs = [tok(512), tok(512), tok(D), tok(D), tok(D)] + [
        _const_spec(a.shape) for a in (wom, wof, wout, g1, g2, w1, w2, g3)]
    return pl.pallas_call(
        _post_kernel,
        grid=(B, S // tm),
        in_specs=in_specs,
        out_specs=tok(D),
        out_shape=jax.ShapeDtypeStruct((B, S, D), F32),
        compiler_params=pltpu.CompilerParams(
            dimension_semantics=("parallel", "parallel"), vmem_limit_bytes=VMEM_LIMIT),
        name="post",
    )(o_mla, o_fox, ga, gb, x, wom, wof, wout, g1, g2, w1, w2, g3)


def _pad_heads(w, width):
    r = w.shape[0]
    w = w.reshape(r, HEADS, width)
    w = jnp.pad(w, ((0, 0), (0, 0), (0, HEAD_PAD - width)))
    return w.reshape(r, HEADS * HEAD_PAD)


def _arrange_in_proj(w, b):
    offs = np.cumsum((0,) + IN_SPLITS)
    cq, ckv, kr, fq, fk, fv, f, ga, gb = [w[:, offs[i]:offs[i + 1]] for i in range(9)]
    bcq, bckv, bkr, bfq, bfk, bfv, bf, bga, bgb = [b[None, offs[i]:offs[i + 1]] for i in range(9)]
    kr_pad = ((0, 0), (MLA_NOPE_DIM, LANES - MLA_NOPE_DIM - MLA_ROPE_DIM))
    f_rep = lambda a: jnp.pad(jnp.repeat(a, N_PIECES, axis=1), ((0, 0), (0, LANES - PIECE_LANES)))
    w_r = jnp.concatenate([cq, ckv, jnp.pad(kr, kr_pad), fq, fk, f_rep(f), ga, gb], axis=1)
    b_r = jnp.concatenate([bcq, bckv, jnp.pad(bkr, kr_pad), bfq, bfk, f_rep(bf), bga, bgb], axis=1)
    return w_r.astype(BF16), b_r.astype(F32), fv.T.astype(BF16), bfv.T.astype(F32)


def _static_tables():
    inv = ROPE_THETA ** (-np.arange(HALF_ROPE, dtype=np.float32) / HALF_ROPE)
    freq = inv.reshape(HALF_ROPE, 1)
    piece = np.full((1, LANES), float(N_PIECES), np.float32)
    piece[0, :PIECE_LANES] = np.arange(PIECE_LANES) % N_PIECES
    vfill = np.zeros((VT_ROWS - HEAD_V, BLK), np.float32)
    vfill[0, :] = 1.0
    return jnp.asarray(freq), jnp.asarray(piece), jnp.asarray(vfill, BF16)


def _layer(x, pos_row, ln_pre_mix, ln_post_mix, ln_pre_mlp, ln_post_mlp, w_in, b_in, q_a_norm,
           w_uq, kv_a_norm, w_uk, w_uv, w_o_mla, w_o_fox, w_out, w_ff1, w_ff2):
    B, S, _ = x.shape
    row = lambda v: v[None, :].astype(F32)
    w_r, b_r, wfvt, bfvt = _arrange_in_proj(w_in, b_in)
    freq, piece, vfill = _static_tables()
    consts = (row(ln_pre_mix), w_r, b_r, row(q_a_norm),
              _pad_heads(w_uq, MLA_NOPE_DIM + MLA_ROPE_DIM).astype(BF16), row(kv_a_norm),
              _pad_heads(w_uk, MLA_NOPE_DIM).astype(BF16), w_uv.T.astype(BF16), wfvt, bfvt,
              freq, piece, vfill)
    (q_mla, k_mla, vt_mla, q_fox, k_fox, vt_fox, goff, ga, gb) = _pre_call(x, pos_row, consts)

    goff = jnp.transpose(goff[:, :, 0, 0:PIECE_LANES:N_PIECES], (0, 2, 1))
    goff = goff.reshape(B * FOX_HEADS, S // BLK)

    o_mla = _attn_call(q_mla, k_mla, vt_mla)
    o_fox = _attn_call(q_fox, k_fox, vt_fox, goff)

    return _post_call(o_mla, o_fox, ga, gb, x, w_o_mla.astype(BF16), w_o_fox.astype(BF16),
                      w_out.astype(BF16), row(ln_post_mix), row(ln_pre_mlp),
                      w_ff1.astype(BF16), w_ff2.astype(BF16), row(ln_post_mlp))


def kernel(x, positions, ln_pre_mix, ln_post_mix, ln_pre_mlp, ln_post_mlp, w_in, b_in,
           q_a_norm, w_uq, kv_a_norm, w_uk, w_uv, w_o_mla, w_o_fox, w_out, w_ff1, w_ff2):
    pos_row = positions[:, None, :]
    for l in range(w_in.shape[0]):
        x = _layer(x, pos_row, ln_pre_mix[l], ln_post_mix[l], ln_pre_mlp[l], ln_post_mlp[l],
                   w_in[l], b_in[l], q_a_norm[l], w_uq[l], kv_a_norm[l], w_uk[l], w_uv[l],
                   w_o_mla[l], w_o_fox[l], w_out[l], w_ff1[l], w_ff2[l])
    return x
```

```python
import functools

import numpy as np
import jax
import jax.numpy as jnp
from jax import lax
from jax.experimental import pallas as pl
from jax.experimental.pallas import tpu as pltpu

F32 = jnp.float32
BF16 = jnp.bfloat16

D_MODEL = 1024
MLA_HEADS = 8
MLA_Q_LORA = 256
MLA_KV_LORA = 128
MLA_NOPE_DIM = 64
MLA_ROPE_DIM = 32
MLA_V_DIM = 64
FOX_HEADS = 8
FOX_HEAD_DIM = 64
D_FF = 4 * D_MODEL
ROPE_THETA = 10000.0
NORM_EPS = 1e-6
IN_SPLITS = (MLA_Q_LORA, MLA_KV_LORA, MLA_ROPE_DIM,
             FOX_HEADS * FOX_HEAD_DIM, FOX_HEADS * FOX_HEAD_DIM, FOX_HEADS * FOX_HEAD_DIM,
             FOX_HEADS, D_MODEL, D_MODEL)

LANES = 128
HEAD_PAD = 128
HEADS = 8
HEAD_V = 64
VT_ROWS = HEAD_V + 64
HALF_ROPE = MLA_ROPE_DIM // 2
N_PAIRS = HEADS // 2
AUG = 64
N_PIECES = 3
PIECE_LANES = N_PIECES * FOX_HEADS
LOG2E = float(np.log2(np.e))
SKIP_BITS = 190.0
NORM_MARGIN = 1.02
VMEM_LIMIT = 52 * 1024 * 1024

BLK = 512
POST_TM = 512
FF_CHUNK = 512

_W = (MLA_Q_LORA, MLA_KV_LORA, LANES, FOX_HEADS * FOX_HEAD_DIM, FOX_HEADS * FOX_HEAD_DIM, LANES,
      D_MODEL, D_MODEL)
_OFF = tuple(int(v) for v in np.cumsum((0,) + _W))
(C_CQ, C_CKV, C_KR, C_FQ, C_FK, C_F, C_GA, C_GB) = tuple(
    (_OFF[i], _OFF[i + 1]) for i in range(8))


def _rms(x, g):
    return x * lax.rsqrt(jnp.mean(x * x, axis=-1, keepdims=True) + NORM_EPS) * g


def _sigmoid(x):
    return 0.5 * jnp.tanh(0.5 * x) + 0.5


def _const_spec(shape):
    zeros = (0,) * len(shape)
    return pl.BlockSpec(shape, lambda *_: zeros, pipeline_mode=pl.Buffered(1))


def _dot_nt(a, b):
    return lax.dot_general(a, b, (((1,), (1,)), ((), ())), preferred_element_type=F32)


def _pre_kernel(x_ref, pos_ref, g_ref, w_ref, b_ref, qn_ref, wuq_ref, kvn_ref, wuk_ref,
                wuvt_ref, wfvt_ref, bfvt_ref, freq_ref, piece_ref, vfill_ref, ind_ref,
                qm_ref, km_ref, vmt_ref, qf_ref, kf_ref, vft_ref, goff_ref, norm_ref,
                ga_ref, gb_ref,
                carry_ref):
    tm = x_ref.shape[1]
    lane = lax.broadcasted_iota(jnp.int32, (tm, LANES), 1)

    @pl.when(pl.program_id(1) == 0)
    def _():
        carry_ref[...] = jnp.zeros_like(carry_ref)

    h = _rms(x_ref[0], g_ref[...]).astype(BF16)

    def proj(c):
        return (jnp.dot(h, w_ref[:, c[0]:c[1]], preferred_element_type=F32)
                + b_ref[:, c[0]:c[1]])

    ang = freq_ref[...] * pos_ref[0].astype(F32)
    cos_t = jnp.cos(ang)
    sin_t = jnp.sin(ang)
    below = (MLA_NOPE_DIM, tm)
    above = (LANES - MLA_NOPE_DIM - MLA_ROPE_DIM, tm)
    cos = jnp.transpose(jnp.concatenate(
        [jnp.ones(below, F32), cos_t, cos_t, jnp.ones(above, F32)], axis=0))
    sin = jnp.transpose(jnp.concatenate(
        [jnp.zeros(below, F32), -sin_t, sin_t, jnp.zeros(above, F32)], axis=0))
    s_lo = jnp.where(lane < MLA_NOPE_DIM + HALF_ROPE, sin, 0.0)
    s_hi = jnp.where(lane >= MLA_NOPE_DIM + HALF_ROPE, sin, 0.0)

    def rope(blk):
        return (blk * cos + pltpu.roll(blk, LANES - HALF_ROPE, 1) * s_lo
                + pltpu.roll(blk, HALF_ROPE, 1) * s_hi)

    c_q = _rms(proj(C_CQ), qn_ref[...]).astype(BF16)
    q_full = jnp.dot(c_q, wuq_ref[...], preferred_element_type=F32)
    q_scale = (MLA_NOPE_DIM + MLA_ROPE_DIM) ** -0.5 * LOG2E
    c_kv = _rms(proj(C_CKV), kvn_ref[...]).astype(BF16)
    k_nope = jnp.dot(c_kv, wuk_ref[...], preferred_element_type=F32)
    k_rot = rope(proj(C_KR))
    for hd in range(HEADS):
        sl = slice(hd * HEAD_PAD, (hd + 1) * HEAD_PAD)
        qm_ref[0, :, sl] = (rope(q_full[:, sl]) * q_scale).astype(BF16)
        km_ref[0, :, sl] = (k_nope[:, sl] + k_rot).astype(BF16)
    vmt = _dot_nt(wuvt_ref[...], c_kv).astype(BF16)

    piece_id = piece_ref[...]

    def pieces_by_lane(v):
        p0 = v.astype(BF16)
        r1 = v - p0.astype(F32)
        p1 = r1.astype(BF16)
        p2 = (r1 - p1.astype(F32)).astype(BF16)
        return jnp.where(piece_id == 0.0, p0, jnp.where(piece_id == 1.0, p1, p2))

    f_logit = proj(C_F)
    log_f = jnp.minimum(f_logit, 0.0) - jnp.log1p(jnp.exp(-jnp.abs(f_logit)))
    row = lax.broadcasted_iota(jnp.int32, (tm, tm), 0)
    col = lax.broadcasted_iota(jnp.int32, (tm, tm), 1)
    tri = (row >= col).astype(BF16)
    cum = jnp.dot(tri, pieces_by_lane(log_f), preferred_element_type=F32)
    cum = cum + pltpu.roll(cum, LANES - 1, 1) + pltpu.roll(cum, LANES - 2, 1)
    local = jnp.where(piece_id == 0.0, cum, 0.0)
    goff_ref[0, 0] = carry_ref[...] * LOG2E
    carry_ref[...] = carry_ref[...] + local[tm - 1:tm, :]

    local3 = local + pltpu.roll(local, 1, 1) + pltpu.roll(local, 2, 1)
    aug = pieces_by_lane(local3 * LOG2E).astype(F32)

    fq = proj(C_FQ) * (FOX_HEAD_DIM ** -0.5 * LOG2E)
    fk = proj(C_FK)

    def max_row_norm(v):
        sq = jnp.dot((v * v).astype(BF16), ind_ref[...], preferred_element_type=F32)
        return jnp.sqrt(jnp.max(sq, axis=0, keepdims=True)) * NORM_MARGIN

    norm_ref[0, 0] = jnp.concatenate(
        [max_row_norm(fq), max_row_norm(fk), jnp.zeros((6, LANES), F32)], axis=0)
    feat = lane < FOX_HEAD_DIM
    aug_a = lane < AUG + N_PIECES
    aug_b = lane < AUG + 2 * N_PIECES
    for hd in range(HEADS):
        src = slice((hd // 2) * LANES, (hd // 2 + 1) * LANES)
        dst = slice(hd * HEAD_PAD, (hd + 1) * HEAD_PAD)
        q_blk, k_blk = fq[:, src], fk[:, src]
        if hd % 2:
            q_blk = pltpu.roll(q_blk, FOX_HEAD_DIM, 1)
            k_blk = pltpu.roll(k_blk, FOX_HEAD_DIM, 1)
        a_t = pltpu.roll(aug, AUG - N_PIECES * hd, 1)
        b_s = pltpu.roll(-aug, AUG + N_PIECES - N_PIECES * hd, 1)
        qf_ref[0, :, dst] = jnp.where(
            feat, q_blk, jnp.where(aug_a, a_t, jnp.where(aug_b, 1.0, 0.0))).astype(BF16)
        kf_ref[0, :, dst] = jnp.where(
            feat, k_blk, jnp.where(aug_a, 1.0, jnp.where(aug_b, b_s, 0.0))).astype(BF16)
    vft = (_dot_nt(wfvt_ref[...], h) + bfvt_ref[...]).astype(BF16)

    vfill = vfill_ref[...]
    for hd in range(HEADS):
        rows = slice(hd * HEAD_V, (hd + 1) * HEAD_V)
        vmt_ref[0, hd, 0, 0:HEAD_V, :] = vmt[rows, :]
        vmt_ref[0, hd, 0, HEAD_V:, :] = vfill
        vft_ref[0, hd, 0, 0:HEAD_V, :] = vft[rows, :]
        vft_ref[0, hd, 0, HEAD_V:, :] = vfill

    ga_ref[0] = _sigmoid(proj(C_GA)).astype(BF16)
    gb_ref[0] = _sigmoid(proj(C_GB)).astype(BF16)


def _pre_call(x, pos_row, consts):
    B, S, D = x.shape
    tm = BLK
    nblk = S // tm
    tok = lambda n: pl.BlockSpec((1, tm, n), lambda b, j: (b, j, 0))
    vt_shape = jax.ShapeDtypeStruct((B, HEADS, nblk, VT_ROWS, tm), BF16)
    vt_spec = pl.BlockSpec((1, HEADS, 1, VT_ROWS, tm), lambda b, j: (b, 0, j, 0, 0))
    out_shape = [
        jax.ShapeDtypeStruct((B, S, HEADS * HEAD_PAD), BF16),
        jax.ShapeDtypeStruct((B, S, HEADS * HEAD_PAD), BF16),
        vt_shape,
        jax.ShapeDtypeStruct((B, S, HEADS * HEAD_PAD), BF16),
        jax.ShapeDtypeStruct((B, S, HEADS * HEAD_PAD), BF16),
        vt_shape,
        jax.ShapeDtypeStruct((B, nblk, 1, LANES), F32),
        jax.ShapeDtypeStruct((B, nblk, 8, LANES), F32),
        jax.ShapeDtypeStruct((B, S, D), BF16),
        jax.ShapeDtypeStruct((B, S, D), BF16),
    ]
    out_specs = [tok(HEADS * HEAD_PAD), tok(HEADS * HEAD_PAD), vt_spec,
                 tok(HEADS * HEAD_PAD), tok(HEADS * HEAD_PAD), vt_spec,
                 pl.BlockSpec((1, 1, 1, LANES), lambda b, j: (b, j, 0, 0)),
                 pl.BlockSpec((1, 1, 8, LANES), lambda b, j: (b, j, 0, 0)),
                 tok(D), tok(D)]
    in_specs = ([tok(D), pl.BlockSpec((1, 1, tm), lambda b, j: (b, 0, j))]
                + [_const_spec(a.shape) for a in consts])
    return pl.pallas_call(
        _pre_kernel,
        grid=(B, nblk),
        in_specs=in_specs,
        out_specs=out_specs,
        out_shape=out_shape,
        scratch_shapes=[pltpu.VMEM((1, LANES), F32)],
        compiler_params=pltpu.CompilerParams(
            dimension_semantics=("parallel", "arbitrary"), vmem_limit_bytes=VMEM_LIMIT),
        name="pre",
    )(x, pos_row, *consts)


def _attn_kernel(*refs, fox, nblk):
    if fox:
        goff_ref, qn_ref, kn_ref = refs[:3]
        refs = refs[3:]
    q_ref, k_ref, vt_ref, o_ref, s_ref, p_ref, mx_ref, m_ref, a_ref, acc_ref = refs
    t = BLK
    half = t // 2
    g_base = (pl.program_id(0) * N_PAIRS + pl.program_id(1)) * 2

    def stage_q(i, j, diag):
        qoff = pl.multiple_of(i * t, t)
        koff = pl.multiple_of(j * t, t)
        for hd in range(2):
            lanes = slice(hd * HEAD_PAD, (hd + 1) * HEAD_PAD)
            if not diag:
                s = _dot_nt(k_ref[0, pl.ds(koff, t), lanes], q_ref[0, pl.ds(qoff, t), lanes])
                s_ref[hd] = s
                mx_ref[hd] = jnp.max(s, axis=0, keepdims=True)
                continue
            causal = (lax.broadcasted_iota(jnp.int32, (half, half), 0)
                      <= lax.broadcasted_iota(jnp.int32, (half, half), 1))
            k_lo = k_ref[0, pl.ds(koff, half), lanes]
            k_hi = k_ref[0, pl.ds(pl.multiple_of(koff + half, half), half), lanes]
            s_lo = _dot_nt(k_lo, q_ref[0, pl.ds(qoff, t), lanes])
            s_hh = _dot_nt(k_hi, q_ref[0, pl.ds(pl.multiple_of(qoff + half, half), half), lanes])
            s_ll = jnp.where(causal, s_lo[:, :half], -jnp.inf)
            s_hh = jnp.where(causal, s_hh, -jnp.inf)
            s_ref[hd, :half, :half] = s_ll
            s_ref[hd, :half, half:] = s_lo[:, half:]
            s_ref[hd, half:, :half] = jnp.full((half, half), -jnp.inf, F32)
            s_ref[hd, half:, half:] = s_hh
            mx_ref[hd, :, :half] = jnp.max(s_ll, axis=0, keepdims=True)
            mx_ref[hd, :, half:] = jnp.maximum(jnp.max(s_lo[:, half:], axis=0, keepdims=True),
                                               jnp.max(s_hh, axis=0, keepdims=True))

    def stage_e(i, j):
        for hd in range(2):
            m_prev = m_ref[hd]
            if fox:
                delta = goff_ref[g_base + hd, i] - goff_ref[g_base + hd, j]
                m_new = jnp.maximum(m_prev, mx_ref[hd] + delta)
                shift = m_new - delta
            else:
                m_new = jnp.maximum(m_prev, mx_ref[hd])
                shift = m_new
            p_ref[hd] = jnp.exp2(s_ref[hd] - shift).astype(BF16)
            a_ref[hd] = jnp.exp2(m_prev - m_new)
            m_ref[hd] = m_new

    def stage_p(j, diag=False):
        for hd in range(2):
            if not diag:
                acc_ref[hd] = a_ref[hd] * acc_ref[hd] + jnp.dot(
                    vt_ref[0, hd, j], p_ref[hd], preferred_element_type=F32)
                continue
            acc_ref[hd, :, :half] = a_ref[hd, :, :half] * acc_ref[hd, :, :half] + jnp.dot(
                vt_ref[0, hd, j, :, :half], p_ref[hd, :half, :half], preferred_element_type=F32)
            acc_ref[hd, :, half:] = a_ref[hd, :, half:] * acc_ref[hd, :, half:] + jnp.dot(
                vt_ref[0, hd, j], p_ref[hd, :, half:], preferred_element_type=F32)

    def finish(i):
        outs = []
        for hd in range(2):
            acc = acc_ref[hd]
            outs.append(acc[0:HEAD_V, :] / acc[HEAD_V:HEAD_V + 1, :])
        o_ref[0, pl.ds(pl.multiple_of(i * t, t), t), :] = jnp.transpose(
            jnp.concatenate(outs, axis=0)).astype(o_ref.dtype)
        reset()

    def reset():
        m_ref[...] = jnp.full(m_ref.shape, -jnp.inf, F32)
        acc_ref[...] = jnp.zeros_like(acc_ref)

    def step(i, j, nxt, next_diag=False, row_start=False):
        if row_start:
            stage_p(i - 1, diag=True)
            finish(i - 1)
        else:
            stage_p(j - 1)
        stage_e(i, j)
        if nxt is not None:
            stage_q(nxt[0], nxt[1], next_diag)

    def plain_steps(i, j0, count):
        for u in range(count):
            step(i, j0 + u, (i, j0 + u + 1))

    def plain_run(i, j0, n_plain):
        def pair(_, j):
            plain_steps(i, j, 2)
            return j + 2

        j = lax.fori_loop(0, lax.shift_right_logical(n_plain, 1), pair, j0)

        @pl.when(lax.bitwise_and(n_plain, 1) != 0)
        def _():
            plain_steps(i, j, 1)

    def first_tile(i):
        if not fox:
            return jnp.int32(0)

        def negligible(j):
            ok = None
            for hd in range(2):
                r = g_base + hd
                bound = (goff_ref[r, i] - goff_ref[r, j + 1]
                         + qn_ref[r, i] * (kn_ref[r, j] + kn_ref[r, i]))
                small = bound < -SKIP_BITS
                ok = small if ok is None else jnp.logical_and(ok, small)
            return ok

        return lax.while_loop(lambda j: jnp.logical_and(j < i - 2, negligible(j)),
                              lambda j: j + 1, jnp.int32(0))

    last = nblk - 1
    reset()
    stage_q(0, 0, True)
    stage_e(0, 0)
    stage_q(1, 0, False)
    step(1, 0, (1, 1), next_diag=True, row_start=True)
    step(1, 1, (2, 0))
    step(2, 0, (2, 1), row_start=True)

    def rest_of_row(i, first):
        plain_run(i, first + 1, i - 2 - first)
        step(i, i - 1, (i, i), next_diag=True)

    def row(i, first):
        rest_of_row(i, first)
        nxt_first = first_tile(i + 1)
        step(i, i, (i + 1, nxt_first))
        step(i + 1, nxt_first, (i + 1, nxt_first + 1), row_start=True)
        return nxt_first

    first = lax.fori_loop(2, last, row, jnp.int32(0))
    rest_of_row(last, first)
    step(last, last, None)
    stage_p(last, diag=True)
    finish(last)


def _attn_call(q, k, vt, tables=None):
    B, S, _ = q.shape
    fox = tables is not None
    t = BLK
    nblk = S // t
    assert nblk >= 2
    in_specs = [
        pl.BlockSpec((1, S, 2 * HEAD_PAD), lambda b, p: (b, 0, p)),
        pl.BlockSpec((1, S, 2 * HEAD_PAD), lambda b, p: (b, 0, p)),
        pl.BlockSpec((1, 2, nblk, VT_ROWS, t), lambda b, p: (b, p, 0, 0, 0)),
    ]
    args = [q, k, vt]
    if fox:
        in_specs = [pl.BlockSpec(memory_space=pltpu.SMEM)] * len(tables) + in_specs
        args = list(tables) + args
    stat = pltpu.VMEM((2, 1, t), F32)
    return pl.pallas_call(
        functools.partial(_attn_kernel, fox=fox, nblk=nblk),
        grid=(B, N_PAIRS),
        in_specs=in_specs,
        out_specs=pl.BlockSpec((1, S, 2 * HEAD_V), lambda b, p: (b, 0, p)),
        out_shape=jax.ShapeDtypeStruct((B, S, HEADS * HEAD_V), BF16),
        scratch_shapes=[pltpu.VMEM((2, t, t), F32), pltpu.VMEM((2, t, t), BF16),
                        stat, stat, stat, pltpu.VMEM((2, VT_ROWS, t), F32)],
        compiler_params=pltpu.CompilerParams(
            dimension_semantics=("parallel", "parallel"),
            vmem_limit_bytes=VMEM_LIMIT),
        name="attn_fox" if fox else "attn_mla",
    )(*args)


def _post_kernel(om_ref, of_ref, ga_ref, gb_ref, x_ref, wom_ref, wof_ref, wout_ref,
                 g1_ref, g2_ref, w1_ref, w2_ref, g3_ref, out_ref):
    y_mla = jnp.dot(om_ref[0], wom_ref[...], preferred_element_type=F32)
    y_fox = jnp.dot(of_ref[0], wof_ref[...], preferred_element_type=F32)
    merged = ga_ref[0].astype(F32) * y_mla + gb_ref[0].astype(F32) * y_fox
    mix = jnp.dot(merged.astype(BF16), wout_ref[...], preferred_element_type=F32)
    x1 = x_ref[0] + _rms(mix, g1_ref[...])
    h2 = _rms(x1, g2_ref[...]).astype(BF16)
    m = jnp.zeros_like(x1)
    for c in range(D_FF // FF_CHUNK):
        sl = slice(c * FF_CHUNK, (c + 1) * FF_CHUNK)
        u = jnp.dot(h2, w1_ref[:, sl], preferred_element_type=F32)
        a = jnp.square(jnp.maximum(u, 0.0)).astype(BF16)
        m = m + jnp.dot(a, w2_ref[sl, :], preferred_element_type=F32)
    out_ref[0] = x1 + _rms(m, g3_ref[...])


def _post_call(o_mla, o_fox, ga, gb, x, wom, wof, wout, g1, g2, w1, w2, g3):
    B, S, D = x.shape
    tm = POST_TM
    tok = lambda n: pl.BlockSpec((1, tm, n), lambda b, j: (b, j, 0))
    in_specs = [tok(512), tok(512), tok(D), tok(D), tok(D)] + [
        _const_spec(a.shape) for a in (wom, wof, wout, g1, g2, w1, w2, g3)]
    return pl.pallas_call(
        _post_kernel,
        grid=(B, S // tm),
        in_specs=in_specs,
        out_specs=tok(D),
        out_shape=jax.ShapeDtypeStruct((B, S, D), F32),
        compiler_params=pltpu.CompilerParams(
            dimension_semantics=("parallel", "parallel"), vmem_limit_bytes=VMEM_LIMIT),
        name="post",
    )(o_mla, o_fox, ga, gb, x, wom, wof, wout, g1, g2, w1, w2, g3)


def _pad_heads(w, width):
    r = w.shape[0]
    w = w.reshape(r, HEADS, width)
    w = jnp.pad(w, ((0, 0), (0, 0), (0, HEAD_PAD - width)))
    return w.reshape(r, HEADS * HEAD_PAD)


def _arrange_in_proj(w, b):
    offs = np.cumsum((0,) + IN_SPLITS)
    cq, ckv, kr, fq, fk, fv, f, ga, gb = [w[:, offs[i]:offs[i + 1]] for i in range(9)]
    bcq, bckv, bkr, bfq, bfk, bfv, bf, bga, bgb = [b[None, offs[i]:offs[i + 1]] for i in range(9)]
    kr_pad = ((0, 0), (MLA_NOPE_DIM, LANES - MLA_NOPE_DIM - MLA_ROPE_DIM))
    f_rep = lambda a: jnp.pad(jnp.repeat(a, N_PIECES, axis=1), ((0, 0), (0, LANES - PIECE_LANES)))
    w_r = jnp.concatenate([cq, ckv, jnp.pad(kr, kr_pad), fq, fk, f_rep(f), ga, gb], axis=1)
    b_r = jnp.concatenate([bcq, bckv, jnp.pad(bkr, kr_pad), bfq, bfk, f_rep(bf), bga, bgb], axis=1)
    return w_r.astype(BF16), b_r.astype(F32), fv.T.astype(BF16), bfv.T.astype(F32)


def _static_tables():
    inv = ROPE_THETA ** (-np.arange(HALF_ROPE, dtype=np.float32) / HALF_ROPE)
    freq = inv.reshape(HALF_ROPE, 1)
    piece = np.full((1, LANES), float(N_PIECES), np.float32)
    piece[0, :PIECE_LANES] = np.arange(PIECE_LANES) % N_PIECES
    vfill = np.zeros((VT_ROWS - HEAD_V, BLK), np.float32)
    vfill[0, :] = 1.0
    ind = np.zeros((FOX_HEADS * FOX_HEAD_DIM, LANES), np.float32)
    for hd in range(FOX_HEADS):
        ind[hd * FOX_HEAD_DIM:(hd + 1) * FOX_HEAD_DIM, hd] = 1.0
    return (jnp.asarray(freq), jnp.asarray(piece), jnp.asarray(vfill, BF16),
            jnp.asarray(ind, BF16))


def _layer(x, pos_row, ln_pre_mix, ln_post_mix, ln_pre_mlp, ln_post_mlp, w_in, b_in, q_a_norm,
           w_uq, kv_a_norm, w_uk, w_uv, w_o_mla, w_o_fox, w_out, w_ff1, w_ff2):
    B, S, _ = x.shape
    row = lambda v: v[None, :].astype(F32)
    w_r, b_r, wfvt, bfvt = _arrange_in_proj(w_in, b_in)
    freq, piece, vfill, ind = _static_tables()
    consts = (row(ln_pre_mix), w_r, b_r, row(q_a_norm),
              _pad_heads(w_uq, MLA_NOPE_DIM + MLA_ROPE_DIM).astype(BF16), row(kv_a_norm),
              _pad_heads(w_uk, MLA_NOPE_DIM).astype(BF16), w_uv.T.astype(BF16), wfvt, bfvt,
              freq, piece, vfill, ind)
    (q_mla, k_mla, vt_mla, q_fox, k_fox, vt_fox, goff, norms, ga, gb) = _pre_call(
        x, pos_row, consts)

    per_head = lambda a: jnp.transpose(a, (0, 2, 1)).reshape(B * FOX_HEADS, S // BLK)
    tables = (per_head(goff[:, :, 0, 0:PIECE_LANES:N_PIECES]),
              per_head(norms[:, :, 0, :FOX_HEADS]), per_head(norms[:, :, 1, :FOX_HEADS]))

    o_mla = _attn_call(q_mla, k_mla, vt_mla)
    o_fox = _attn_call(q_fox, k_fox, vt_fox, tables)

    return _post_call(o_mla, o_fox, ga, gb, x, w_o_mla.astype(BF16), w_o_fox.astype(BF16),
                      w_out.astype(BF16), row(ln_post_mix), row(ln_pre_mlp),
                      w_ff1.astype(BF16), w_ff2.astype(BF16), row(ln_post_mlp))


def kernel(x, positions, ln_pre_mix, ln_post_mix, ln_pre_mlp, ln_post_mlp, w_in, b_in,
           q_a_norm, w_uq, kv_a_norm, w_uk, w_uv, w_o_mla, w_o_fox, w_out, w_ff1, w_ff2):
    pos_row = positions[:, None, :]
    for l in range(w_in.shape[0]):
        x = _layer(x, pos_row, ln_pre_mix[l], ln_post_mix[l], ln_pre_mlp[l], ln_post_mlp[l],
                   w_in[l], b_in[l], q_a_norm[l], w_uq[l], kv_a_norm[l], w_uk[l], w_uv[l],
                   w_o_mla[l], w_o_fox[l], w_out[l], w_ff1[l], w_ff2[l])
    return x
```

```python
import functools

import numpy as np
import jax
import jax.numpy as jnp
from jax import lax
from jax.experimental import pallas as pl
from jax.experimental.pallas import tpu as pltpu

F32 = jnp.float32
BF16 = jnp.bfloat16

D_MODEL = 1024
MLA_HEADS = 8
MLA_Q_LORA = 256
MLA_KV_LORA = 128
MLA_NOPE_DIM = 64
MLA_ROPE_DIM = 32
MLA_V_DIM = 64
FOX_HEADS = 8
FOX_HEAD_DIM = 64
D_FF = 4 * D_MODEL
ROPE_THETA = 10000.0
NORM_EPS = 1e-6
IN_SPLITS = (MLA_Q_LORA, MLA_KV_LORA, MLA_ROPE_DIM,
             FOX_HEADS * FOX_HEAD_DIM, FOX_HEADS * FOX_HEAD_DIM, FOX_HEADS * FOX_HEAD_DIM,
             FOX_HEADS, D_MODEL, D_MODEL)

LANES = 128
HEAD_PAD = 128
HEADS = 8
HEAD_V = 64
VT_ROWS = HEAD_V + 64
HALF_ROPE = MLA_ROPE_DIM // 2
N_PAIRS = HEADS // 2
AUG = 64
N_PIECES = 3
PIECE_LANES = N_PIECES * FOX_HEADS
LOG2E = float(np.log2(np.e))
SKIP_BITS = 160.0
NORM_MARGIN = 1.02
VMEM_LIMIT = 52 * 1024 * 1024

BLK = 512
POST_TM = 512
FF_CHUNK = 512

_W = (MLA_Q_LORA, MLA_KV_LORA, LANES, FOX_HEADS * FOX_HEAD_DIM, FOX_HEADS * FOX_HEAD_DIM, LANES,
      D_MODEL, D_MODEL)
_OFF = tuple(int(v) for v in np.cumsum((0,) + _W))
(C_CQ, C_CKV, C_KR, C_FQ, C_FK, C_F, C_GA, C_GB) = tuple(
    (_OFF[i], _OFF[i + 1]) for i in range(8))


def _rms(x, g):
    return x * lax.rsqrt(jnp.mean(x * x, axis=-1, keepdims=True) + NORM_EPS) * g


def _sigmoid_of_twice(half_x):
    return 0.5 * jnp.tanh(half_x) + 0.5


def _const_spec(shape):
    zeros = (0,) * len(shape)
    return pl.BlockSpec(shape, lambda *_: zeros, pipeline_mode=pl.Buffered(1))


def _dot_nt(a, b):
    return lax.dot_general(a, b, (((1,), (1,)), ((), ())), preferred_element_type=F32)


def _pre_kernel(x_ref, pos_ref, g_ref, w_ref, b_ref, qn_ref, wuq_ref, kvn_ref, wuk_ref,
                wuvt_ref, wfvt_ref, bfvt_ref, freq_ref, piece_ref, vfill_ref, ind_ref,
                qm_ref, km_ref, vmt_ref, qf_ref, kf_ref, vft_ref, goff_ref, norm_ref,
                ga_ref, gb_ref,
                carry_ref):
    tm = x_ref.shape[1]
    lane = lax.broadcasted_iota(jnp.int32, (tm, LANES), 1)

    @pl.when(pl.program_id(1) == 0)
    def _():
        carry_ref[...] = jnp.zeros_like(carry_ref)

    h = _rms(x_ref[0], g_ref[...]).astype(BF16)

    def proj(c):
        return (jnp.dot(h, w_ref[:, c[0]:c[1]], preferred_element_type=F32)
                + b_ref[:, c[0]:c[1]])

    ang = freq_ref[...] * pos_ref[0].astype(F32)
    cos_t = jnp.cos(ang)
    sin_t = jnp.sin(ang)
    below = (MLA_NOPE_DIM, tm)
    above = (LANES - MLA_NOPE_DIM - MLA_ROPE_DIM, tm)
    cos = jnp.transpose(jnp.concatenate(
        [jnp.ones(below, F32), cos_t, cos_t, jnp.ones(above, F32)], axis=0))
    sin = jnp.transpose(jnp.concatenate(
        [jnp.zeros(below, F32), -sin_t, sin_t, jnp.zeros(above, F32)], axis=0))
    s_lo = jnp.where(lane < MLA_NOPE_DIM + HALF_ROPE, sin, 0.0)
    s_hi = jnp.where(lane >= MLA_NOPE_DIM + HALF_ROPE, sin, 0.0)

    def rope(blk, tables):
        c, lo, hi = tables
        return (blk * c + pltpu.roll(blk, LANES - HALF_ROPE, 1) * lo
                + pltpu.roll(blk, HALF_ROPE, 1) * hi)

    c_q = _rms(proj(C_CQ), qn_ref[...]).astype(BF16)
    q_full = jnp.dot(c_q, wuq_ref[...], preferred_element_type=F32)
    q_scale = (MLA_NOPE_DIM + MLA_ROPE_DIM) ** -0.5 * LOG2E
    q_tables = (cos * q_scale, s_lo * q_scale, s_hi * q_scale)
    c_kv = _rms(proj(C_CKV), kvn_ref[...]).astype(BF16)
    k_nope = jnp.dot(c_kv, wuk_ref[...], preferred_element_type=F32)
    k_rot = rope(proj(C_KR), (cos, s_lo, s_hi))
    for hd in range(HEADS):
        sl = slice(hd * HEAD_PAD, (hd + 1) * HEAD_PAD)
        qm_ref[0, :, sl] = rope(q_full[:, sl], q_tables).astype(BF16)
        km_ref[0, :, sl] = (k_nope[:, sl] + k_rot).astype(BF16)
    vmt = _dot_nt(wuvt_ref[...], c_kv).astype(BF16)

    piece_id = piece_ref[...]

    def pieces_by_lane(v):
        p0 = v.astype(BF16)
        r1 = v - p0.astype(F32)
        p1 = r1.astype(BF16)
        p2 = (r1 - p1.astype(F32)).astype(BF16)
        return jnp.where(piece_id == 0.0, p0, jnp.where(piece_id == 1.0, p1, p2))

    f_logit = proj(C_F)
    log_f = jnp.minimum(f_logit, 0.0) - jnp.log1p(jnp.exp(-jnp.abs(f_logit)))
    row = lax.broadcasted_iota(jnp.int32, (tm, tm), 0)
    col = lax.broadcasted_iota(jnp.int32, (tm, tm), 1)
    tri = (row >= col).astype(BF16)
    cum = jnp.dot(tri, pieces_by_lane(log_f), preferred_element_type=F32)
    cum = cum + pltpu.roll(cum, LANES - 1, 1) + pltpu.roll(cum, LANES - 2, 1)
    local = jnp.where(piece_id == 0.0, cum, 0.0)
    goff_ref[0, 0] = carry_ref[...] * LOG2E
    carry_ref[...] = carry_ref[...] + local[tm - 1:tm, :]

    local3 = local + pltpu.roll(local, 1, 1) + pltpu.roll(local, 2, 1)
    aug = pieces_by_lane(local3 * LOG2E).astype(F32)

    fq = proj(C_FQ) * (FOX_HEAD_DIM ** -0.5 * LOG2E)
    fk = proj(C_FK)

    def max_row_norm(v):
        sq = jnp.dot((v * v).astype(BF16), ind_ref[...], preferred_element_type=F32)
        return jnp.sqrt(jnp.max(sq, axis=0, keepdims=True)) * NORM_MARGIN

    norm_ref[0, 0] = jnp.concatenate(
        [max_row_norm(fq), max_row_norm(fk), jnp.zeros((6, LANES), F32)], axis=0)
    feat = lane < FOX_HEAD_DIM
    aug_a = jnp.logical_and(lane >= AUG, lane < AUG + N_PIECES)
    aug_b = jnp.logical_and(lane >= AUG + N_PIECES, lane < AUG + 2 * N_PIECES)
    ones_a = jnp.where(aug_a, 1.0, 0.0)
    ones_b = jnp.where(aug_b, 1.0, 0.0)
    for hd in range(HEADS):
        src = slice((hd // 2) * LANES, (hd // 2 + 1) * LANES)
        dst = slice(hd * HEAD_PAD, (hd + 1) * HEAD_PAD)
        q_blk, k_blk = fq[:, src], fk[:, src]
        if hd % 2:
            q_blk = pltpu.roll(q_blk, FOX_HEAD_DIM, 1)
            k_blk = pltpu.roll(k_blk, FOX_HEAD_DIM, 1)
        a_t = pltpu.roll(aug, AUG - N_PIECES * hd, 1)
        b_s = pltpu.roll(-aug, AUG + N_PIECES - N_PIECES * hd, 1)
        qf_ref[0, :, dst] = jnp.where(feat, q_blk, jnp.where(aug_a, a_t, ones_b)).astype(BF16)
        kf_ref[0, :, dst] = jnp.where(feat, k_blk, jnp.where(aug_b, b_s, ones_a)).astype(BF16)
    vft = (_dot_nt(wfvt_ref[...], h) + bfvt_ref[...]).astype(BF16)

    vfill = vfill_ref[...]
    for hd in range(HEADS):
        rows = slice(hd * HEAD_V, (hd + 1) * HEAD_V)
        vmt_ref[0, hd, 0, 0:HEAD_V, :] = vmt[rows, :]
        vmt_ref[0, hd, 0, HEAD_V:, :] = vfill
        vft_ref[0, hd, 0, 0:HEAD_V, :] = vft[rows, :]
        vft_ref[0, hd, 0, HEAD_V:, :] = vfill

    ga_ref[0] = _sigmoid_of_twice(proj(C_GA)).astype(BF16)
    gb_ref[0] = _sigmoid_of_twice(proj(C_GB)).astype(BF16)


def _pre_call(x, pos_row, consts):
    B, S, D = x.shape
    tm = BLK
    nblk = S // tm
    tok = lambda n: pl.BlockSpec((1, tm, n), lambda b, j: (b, j, 0))
    vt_shape = jax.ShapeDtypeStruct((B, HEADS, nblk, VT_ROWS, tm), BF16)
    vt_spec = pl.BlockSpec((1, HEADS, 1, VT_ROWS, tm), lambda b, j: (b, 0, j, 0, 0))
    out_shape = [
        jax.ShapeDtypeStruct((B, S, HEADS * HEAD_PAD), BF16),
        jax.ShapeDtypeStruct((B, S, HEADS * HEAD_PAD), BF16),
        vt_shape,
        jax.ShapeDtypeStruct((B, S, HEADS * HEAD_PAD), BF16),
        jax.ShapeDtypeStruct((B, S, HEADS * HEAD_PAD), BF16),
        vt_shape,
        jax.ShapeDtypeStruct((B, nblk, 1, LANES), F32),
        jax.ShapeDtypeStruct((B, nblk, 8, LANES), F32),
        jax.ShapeDtypeStruct((B, S, D), BF16),
        jax.ShapeDtypeStruct((B, S, D), BF16),
    ]
    out_specs = [tok(HEADS * HEAD_PAD), tok(HEADS * HEAD_PAD), vt_spec,
                 tok(HEADS * HEAD_PAD), tok(HEADS * HEAD_PAD), vt_spec,
                 pl.BlockSpec((1, 1, 1, LANES), lambda b, j: (b, j, 0, 0)),
                 pl.BlockSpec((1, 1, 8, LANES), lambda b, j: (b, j, 0, 0)),
                 tok(D), tok(D)]
    in_specs = ([tok(D), pl.BlockSpec((1, 1, tm), lambda b, j: (b, 0, j))]
                + [_const_spec(a.shape) for a in consts])
    return pl.pallas_call(
        _pre_kernel,
        grid=(B, nblk),
        in_specs=in_specs,
        out_specs=out_specs,
        out_shape=out_shape,
        scratch_shapes=[pltpu.VMEM((1, LANES), F32)],
        compiler_params=pltpu.CompilerParams(
            dimension_semantics=("parallel", "arbitrary"), vmem_limit_bytes=VMEM_LIMIT),
        name="pre",
    )(x, pos_row, *consts)


def _attn_kernel(*refs, fox, nblk):
    if fox:
        goff_ref, qn_ref, kn_ref = refs[:3]
        refs = refs[3:]
    q_ref, k_ref, vt_ref, o_ref, s_ref, p_ref, mx_ref, m_ref, a_ref, acc_ref = refs
    t = BLK
    half = t // 2
    g_base = (pl.program_id(0) * N_PAIRS + pl.program_id(1)) * 2

    def stage_q(i, j, diag):
        qoff = pl.multiple_of(i * t, t)
        koff = pl.multiple_of(j * t, t)
        for hd in range(2):
            lanes = slice(hd * HEAD_PAD, (hd + 1) * HEAD_PAD)
            if not diag:
                s = _dot_nt(k_ref[0, pl.ds(koff, t), lanes], q_ref[0, pl.ds(qoff, t), lanes])
                s_ref[hd] = s
                mx_ref[hd] = jnp.max(s, axis=0, keepdims=True)
                continue
            causal = (lax.broadcasted_iota(jnp.int32, (half, half), 0)
                      <= lax.broadcasted_iota(jnp.int32, (half, half), 1))
            k_lo = k_ref[0, pl.ds(koff, half), lanes]
            k_hi = k_ref[0, pl.ds(pl.multiple_of(koff + half, half), half), lanes]
            s_lo = _dot_nt(k_lo, q_ref[0, pl.ds(qoff, t), lanes])
            s_hh = _dot_nt(k_hi, q_ref[0, pl.ds(pl.multiple_of(qoff + half, half), half), lanes])
            s_ll = jnp.where(causal, s_lo[:, :half], -jnp.inf)
            s_hh = jnp.where(causal, s_hh, -jnp.inf)
            s_ref[hd, :half, :half] = s_ll
            s_ref[hd, :half, half:] = s_lo[:, half:]
            s_ref[hd, half:, :half] = jnp.full((half, half), -jnp.inf, F32)
            s_ref[hd, half:, half:] = s_hh
            mx_ref[hd, :, :half] = jnp.max(s_ll, axis=0, keepdims=True)
            mx_ref[hd, :, half:] = jnp.maximum(jnp.max(s_lo[:, half:], axis=0, keepdims=True),
                                               jnp.max(s_hh, axis=0, keepdims=True))

    def stage_e(i, j):
        for hd in range(2):
            m_prev = m_ref[hd]
            if fox:
                delta = goff_ref[g_base + hd, i] - goff_ref[g_base + hd, j]
                m_new = jnp.maximum(m_prev, mx_ref[hd] + delta)
                shift = m_new - delta
            else:
                m_new = jnp.maximum(m_prev, mx_ref[hd])
                shift = m_new
            p_ref[hd] = jnp.exp2(s_ref[hd] - shift).astype(BF16)
            a_ref[hd] = jnp.exp2(m_prev - m_new)
            m_ref[hd] = m_new

    def stage_p(j, diag=False):
        for hd in range(2):
            if not diag:
                acc_ref[hd] = a_ref[hd] * acc_ref[hd] + jnp.dot(
                    vt_ref[0, hd, j], p_ref[hd], preferred_element_type=F32)
                continue
            acc_ref[hd, :, :half] = a_ref[hd, :, :half] * acc_ref[hd, :, :half] + jnp.dot(
                vt_ref[0, hd, j, :, :half], p_ref[hd, :half, :half], preferred_element_type=F32)
            acc_ref[hd, :, half:] = a_ref[hd, :, half:] * acc_ref[hd, :, half:] + jnp.dot(
                vt_ref[0, hd, j], p_ref[hd, :, half:], preferred_element_type=F32)

    def finish(i):
        outs = []
        for hd in range(2):
            acc = acc_ref[hd]
            outs.append(acc[0:HEAD_V, :] / acc[HEAD_V:HEAD_V + 1, :])
        o_ref[0, pl.ds(pl.multiple_of(i * t, t), t), :] = jnp.transpose(
            jnp.concatenate(outs, axis=0)).astype(o_ref.dtype)
        reset()

    def reset():
        m_ref[...] = jnp.full(m_ref.shape, -jnp.inf, F32)
        acc_ref[...] = jnp.zeros_like(acc_ref)

    def step(i, j, nxt, next_diag=False, row_start=False):
        if row_start:
            stage_p(i - 1, diag=True)
            finish(i - 1)
        else:
            stage_p(j - 1)
        stage_e(i, j)
        if nxt is not None:
            stage_q(nxt[0], nxt[1], next_diag)

    def plain_steps(i, j0, count):
        for u in range(count):
            step(i, j0 + u, (i, j0 + u + 1))

    def plain_run(i, j0, n_plain):
        def pair(_, j):
            plain_steps(i, j, 2)
            return j + 2

        j = lax.fori_loop(0, lax.shift_right_logical(n_plain, 1), pair, j0)

        @pl.when(lax.bitwise_and(n_plain, 1) != 0)
        def _():
            plain_steps(i, j, 1)

    def first_tile(i):
        if not fox:
            return jnp.int32(0)

        def negligible(j):
            ok = None
            for hd in range(2):
                r = g_base + hd
                bound = (goff_ref[r, i] - goff_ref[r, j + 1]
                         + qn_ref[r, i] * (kn_ref[r, j] + kn_ref[r, i]))
                small = bound < -SKIP_BITS
                ok = small if ok is None else jnp.logical_and(ok, small)
            return ok

        return lax.while_loop(lambda j: jnp.logical_and(j < i - 2, negligible(j)),
                              lambda j: j + 1, jnp.int32(0))

    last = nblk - 1
    reset()
    stage_q(0, 0, True)
    stage_e(0, 0)
    stage_q(1, 0, False)
    step(1, 0, (1, 1), next_diag=True, row_start=True)
    step(1, 1, (2, 0))
    step(2, 0, (2, 1), row_start=True)

    def rest_of_row(i, first):
        plain_run(i, first + 1, i - 2 - first)
        step(i, i - 1, (i, i), next_diag=True)

    def row(i, first):
        rest_of_row(i, first)
        nxt_first = first_tile(i + 1)
        step(i, i, (i + 1, nxt_first))
        step(i + 1, nxt_first, (i + 1, nxt_first + 1), row_start=True)
        return nxt_first

    first = lax.fori_loop(2, last, row, jnp.int32(0))
    rest_of_row(last, first)
    step(last, last, None)
    stage_p(last, diag=True)
    finish(last)


def _attn_call(q, k, vt, tables=None):
    B, S, _ = q.shape
    fox = tables is not None
    t = BLK
    nblk = S // t
    assert nblk >= 2
    in_specs = [
        pl.BlockSpec((1, S, 2 * HEAD_PAD), lambda b, p: (b, 0, p)),
        pl.BlockSpec((1, S, 2 * HEAD_PAD), lambda b, p: (b, 0, p)),
        pl.BlockSpec((1, 2, nblk, VT_ROWS, t), lambda b, p: (b, p, 0, 0, 0)),
    ]
    args = [q, k, vt]
    if fox:
        in_specs = [pl.BlockSpec(memory_space=pltpu.SMEM)] * len(tables) + in_specs
        args = list(tables) + args
    stat = pltpu.VMEM((2, 1, t), F32)
    return pl.pallas_call(
        functools.partial(_attn_kernel, fox=fox, nblk=nblk),
        grid=(B, N_PAIRS),
        in_specs=in_specs,
        out_specs=pl.BlockSpec((1, S, 2 * HEAD_V), lambda b, p: (b, 0, p)),
        out_shape=jax.ShapeDtypeStruct((B, S, HEADS * HEAD_V), BF16),
        scratch_shapes=[pltpu.VMEM((2, t, t), F32), pltpu.VMEM((2, t, t), BF16),
                        stat, stat, stat, pltpu.VMEM((2, VT_ROWS, t), F32)],
        compiler_params=pltpu.CompilerParams(
            dimension_semantics=("parallel", "parallel"),
            vmem_limit_bytes=VMEM_LIMIT),
        name="attn_fox" if fox else "attn_mla",
    )(*args)


def _post_kernel(om_ref, of_ref, ga_ref, gb_ref, x_ref, wom_ref, wof_ref, wout_ref,
                 g1_ref, g2_ref, w1_ref, w2_ref, g3_ref, out_ref):
    y_mla = jnp.dot(om_ref[0], wom_ref[...], preferred_element_type=F32)
    y_fox = jnp.dot(of_ref[0], wof_ref[...], preferred_element_type=F32)
    merged = ga_ref[0].astype(F32) * y_mla + gb_ref[0].astype(F32) * y_fox
    mix = jnp.dot(merged.astype(BF16), wout_ref[...], preferred_element_type=F32)
    x1 = x_ref[0] + _rms(mix, g1_ref[...])
    h2 = _rms(x1, g2_ref[...]).astype(BF16)
    m = jnp.zeros_like(x1)
    for c in range(D_FF // FF_CHUNK):
        sl = slice(c * FF_CHUNK, (c + 1) * FF_CHUNK)
        u = jnp.dot(h2, w1_ref[:, sl], preferred_element_type=F32)
        a = jnp.square(jnp.maximum(u, 0.0)).astype(BF16)
        m = m + jnp.dot(a, w2_ref[sl, :], preferred_element_type=F32)
    out_ref[0] = x1 + _rms(m, g3_ref[...])


def _post_call(o_mla, o_fox, ga, gb, x, wom, wof, wout, g1, g2, w1, w2, g3):
    B, S, D = x.shape
    tm = POST_TM
    tok = lambda n: pl.BlockSpec((1, tm, n), lambda b, j: (b, j, 0))
    in_specs = [tok(512), tok(512), tok(D), tok(D), tok(D)] + [
        _const_spec(a.shape) for a in (wom, wof, wout, g1, g2, w1, w2, g3)]
    return pl.pallas_call(
        _post_kernel,
        grid=(B, S // tm),
        in_specs=in_specs,
        out_specs=tok(D),
        out_shape=jax.ShapeDtypeStruct((B, S, D), F32),
        compiler_params=pltpu.CompilerParams(
            dimension_semantics=("parallel", "parallel"), vmem_limit_bytes=VMEM_LIMIT),
        name="post",
    )(o_mla, o_fox, ga, gb, x, wom, wof, wout, g1, g2, w1, w2, g3)


def _pad_heads(w, width):
    r = w.shape[0]
    w = w.reshape(r, HEADS, width)
    w = jnp.pad(w, ((0, 0), (0, 0), (0, HEAD_PAD - width)))
    return w.reshape(r, HEADS * HEAD_PAD)


def _arrange_in_proj(w, b):
    offs = np.cumsum((0,) + IN_SPLITS)
    cq, ckv, kr, fq, fk, fv, f, ga, gb = [w[:, offs[i]:offs[i + 1]] for i in range(9)]
    bcq, bckv, bkr, bfq, bfk, bfv, bf, bga, bgb = [b[None, offs[i]:offs[i + 1]] for i in range(9)]
    kr_pad = ((0, 0), (MLA_NOPE_DIM, LANES - MLA_NOPE_DIM - MLA_ROPE_DIM))
    f_rep = lambda a: jnp.pad(jnp.repeat(a, N_PIECES, axis=1), ((0, 0), (0, LANES - PIECE_LANES)))
    w_r = jnp.concatenate([cq, ckv, jnp.pad(kr, kr_pad), fq, fk, f_rep(f), 0.5 * ga, 0.5 * gb],
                          axis=1)
    b_r = jnp.concatenate([bcq, bckv, jnp.pad(bkr, kr_pad), bfq, bfk, f_rep(bf),
                           0.5 * bga, 0.5 * bgb], axis=1)
    return w_r.astype(BF16), b_r.astype(F32), fv.T.astype(BF16), bfv.T.astype(F32)


def _static_tables():
    inv = ROPE_THETA ** (-np.arange(HALF_ROPE, dtype=np.float32) / HALF_ROPE)
    freq = inv.reshape(HALF_ROPE, 1)
    piece = np.full((1, LANES), float(N_PIECES), np.float32)
    piece[0, :PIECE_LANES] = np.arange(PIECE_LANES) % N_PIECES
    vfill = np.zeros((VT_ROWS - HEAD_V, BLK), np.float32)
    vfill[0, :] = 1.0
    ind = np.zeros((FOX_HEADS * FOX_HEAD_DIM, LANES), np.float32)
    for hd in range(FOX_HEADS):
        ind[hd * FOX_HEAD_DIM:(hd + 1) * FOX_HEAD_DIM, hd] = 1.0
    return (jnp.asarray(freq), jnp.asarray(piece), jnp.asarray(vfill, BF16),
            jnp.asarray(ind, BF16))


def _layer(x, pos_row, ln_pre_mix, ln_post_mix, ln_pre_mlp, ln_post_mlp, w_in, b_in, q_a_norm,
           w_uq, kv_a_norm, w_uk, w_uv, w_o_mla, w_o_fox, w_out, w_ff1, w_ff2):
    B, S, _ = x.shape
    row = lambda v: v[None, :].astype(F32)
    w_r, b_r, wfvt, bfvt = _arrange_in_proj(w_in, b_in)
    freq, piece, vfill, ind = _static_tables()
    consts = (row(ln_pre_mix), w_r, b_r, row(q_a_norm),
              _pad_heads(w_uq, MLA_NOPE_DIM + MLA_ROPE_DIM).astype(BF16), row(kv_a_norm),
              _pad_heads(w_uk, MLA_NOPE_DIM).astype(BF16), w_uv.T.astype(BF16), wfvt, bfvt,
              freq, piece, vfill, ind)
    (q_mla, k_mla, vt_mla, q_fox, k_fox, vt_fox, goff, norms, ga, gb) = _pre_call(
        x, pos_row, consts)

    per_head = lambda a: jnp.transpose(a, (0, 2, 1)).reshape(B * FOX_HEADS, S // BLK)
    tables = (per_head(goff[:, :, 0, 0:PIECE_LANES:N_PIECES]),
              per_head(norms[:, :, 0, :FOX_HEADS]), per_head(norms[:, :, 1, :FOX_HEADS]))

    o_mla = _attn_call(q_mla, k_mla, vt_mla)
    o_fox = _attn_call(q_fox, k_fox, vt_fox, tables)

    return _post_call(o_mla, o_fox, ga, gb, x, w_o_mla.astype(BF16), w_o_fox.astype(BF16),
                      w_out.astype(BF16), row(ln_post_mix), row(ln_pre_mlp),
                      w_ff1.astype(BF16), w_ff2.astype(BF16), row(ln_post_mlp))


def kernel(x, positions, ln_pre_mix, ln_post_mix, ln_pre_mlp, ln_post_mlp, w_in, b_in,
           q_a_norm, w_uq, kv_a_norm, w_uk, w_uv, w_o_mla, w_o_fox, w_out, w_ff1, w_ff2):
    pos_row = positions[:, None, :]
    for l in range(w_in.shape[0]):
        x = _layer(x, pos_row, ln_pre_mix[l], ln_post_mix[l], ln_pre_mlp[l], ln_post_mlp[l],
                   w_in[l], b_in[l], q_a_norm[l], w_uq[l], kv_a_norm[l], w_uk[l], w_uv[l],
                   w_o_mla[l], w_o_fox[l], w_out[l], w_ff1[l], w_ff2[l])
    return x
```

```python
import functools

import numpy as np
import jax
import jax.numpy as jnp
from jax import lax
from jax.experimental import pallas as pl
from jax.experimental.pallas import tpu as pltpu

F32 = jnp.float32
BF16 = jnp.bfloat16

D_MODEL = 1024
MLA_HEADS = 8
MLA_Q_LORA = 256
MLA_KV_LORA = 128
MLA_NOPE_DIM = 64
MLA_ROPE_DIM = 32
MLA_V_DIM = 64
FOX_HEADS = 8
FOX_HEAD_DIM = 64
D_FF = 4 * D_MODEL
ROPE_THETA = 10000.0
NORM_EPS = 1e-6
IN_SPLITS = (MLA_Q_LORA, MLA_KV_LORA, MLA_ROPE_DIM,
             FOX_HEADS * FOX_HEAD_DIM, FOX_HEADS * FOX_HEAD_DIM, FOX_HEADS * FOX_HEAD_DIM,
             FOX_HEADS, D_MODEL, D_MODEL)

LANES = 128
SUBLANES = 8
HEAD_PAD = 128
HEADS = 8
HEAD_V = 64
VT_ROWS = HEAD_V + 64
HALF_ROPE = MLA_ROPE_DIM // 2
N_PAIRS = HEADS // 2
AUG = 64
N_PIECES = 3
PIECE_LANES = N_PIECES * FOX_HEADS
LOG2E = float(np.log2(np.e))
SKIP_BITS = 160.0
NORM_MARGIN = 1.02
VMEM_LIMIT = 52 * 1024 * 1024

BLK = 512
POST_TM = 512
FF_CHUNK = 512

_W = (MLA_Q_LORA, MLA_KV_LORA, LANES, FOX_HEADS * FOX_HEAD_DIM, FOX_HEADS * FOX_HEAD_DIM, LANES,
      D_MODEL, D_MODEL)
_OFF = tuple(int(v) for v in np.cumsum((0,) + _W))
(C_CQ, C_CKV, C_KR, C_FQ, C_FK, C_F, C_GA, C_GB) = tuple(
    (_OFF[i], _OFF[i + 1]) for i in range(8))


def _rms(x, g):
    return x * lax.rsqrt(jnp.mean(x * x, axis=-1, keepdims=True) + NORM_EPS) * g


def _sigmoid_of_twice(half_x):
    return 0.5 * jnp.tanh(half_x) + 0.5


def _const_spec(shape):
    zeros = (0,) * len(shape)
    return pl.BlockSpec(shape, lambda *_: zeros, pipeline_mode=pl.Buffered(1))


def _dot_nt(a, b):
    return lax.dot_general(a, b, (((1,), (1,)), ((), ())), preferred_element_type=F32)


def _pre_kernel(x_ref, pos_ref, g_ref, w_ref, b_ref, qn_ref, wuq_ref, kvn_ref, wuk_ref,
                wuvt_ref, wfvt_ref, bfvt_ref, freq_ref, piece_ref, vfill_ref, ind_ref,
                qm_ref, km_ref, vmt_ref, qf_ref, kf_ref, vft_ref, goff_ref, norm_ref,
                ga_ref, gb_ref,
                carry_ref):
    tm = x_ref.shape[1]
    lane = lax.broadcasted_iota(jnp.int32, (tm, LANES), 1)

    @pl.when(pl.program_id(1) == 0)
    def _():
        carry_ref[...] = jnp.zeros_like(carry_ref)

    h = _rms(x_ref[0], g_ref[...]).astype(BF16)

    def proj(c):
        return (jnp.dot(h, w_ref[:, c[0]:c[1]], preferred_element_type=F32)
                + b_ref[:, c[0]:c[1]])

    ang = freq_ref[...] * pos_ref[0].astype(F32)
    cos_t = jnp.cos(ang)
    sin_t = jnp.sin(ang)
    below = (MLA_NOPE_DIM, tm)
    above = (LANES - MLA_NOPE_DIM - MLA_ROPE_DIM, tm)
    cos = jnp.transpose(jnp.concatenate(
        [jnp.ones(below, F32), cos_t, cos_t, jnp.ones(above, F32)], axis=0))
    sin = jnp.transpose(jnp.concatenate(
        [jnp.zeros(below, F32), -sin_t, sin_t, jnp.zeros(above, F32)], axis=0))
    s_lo = jnp.where(lane < MLA_NOPE_DIM + HALF_ROPE, sin, 0.0)
    s_hi = jnp.where(lane >= MLA_NOPE_DIM + HALF_ROPE, sin, 0.0)

    def rope(blk, tables):
        c, lo, hi = tables
        return (blk * c + pltpu.roll(blk, LANES - HALF_ROPE, 1) * lo
                + pltpu.roll(blk, HALF_ROPE, 1) * hi)

    c_q = _rms(proj(C_CQ), qn_ref[...]).astype(BF16)
    q_full = jnp.dot(c_q, wuq_ref[...], preferred_element_type=F32)
    q_scale = (MLA_NOPE_DIM + MLA_ROPE_DIM) ** -0.5 * LOG2E
    q_tables = (cos * q_scale, s_lo * q_scale, s_hi * q_scale)
    c_kv = _rms(proj(C_CKV), kvn_ref[...]).astype(BF16)
    k_nope = jnp.dot(c_kv, wuk_ref[...], preferred_element_type=F32)
    k_rot = rope(proj(C_KR), (cos, s_lo, s_hi))
    for hd in range(HEADS):
        sl = slice(hd * HEAD_PAD, (hd + 1) * HEAD_PAD)
        qm_ref[0, :, sl] = rope(q_full[:, sl], q_tables).astype(BF16)
        km_ref[0, :, sl] = (k_nope[:, sl] + k_rot).astype(BF16)
    vmt = _dot_nt(wuvt_ref[...], c_kv).astype(BF16)

    piece_id = piece_ref[...]

    def pieces_by_lane(v):
        p0 = v.astype(BF16)
        r1 = v - p0.astype(F32)
        p1 = r1.astype(BF16)
        p2 = (r1 - p1.astype(F32)).astype(BF16)
        return jnp.where(piece_id == 0.0, p0, jnp.where(piece_id == 1.0, p1, p2))

    f_logit = proj(C_F)
    log_f = jnp.minimum(f_logit, 0.0) - jnp.log1p(jnp.exp(-jnp.abs(f_logit)))
    row = lax.broadcasted_iota(jnp.int32, (tm, tm), 0)
    col = lax.broadcasted_iota(jnp.int32, (tm, tm), 1)
    tri = (row >= col).astype(BF16)
    cum = jnp.dot(tri, pieces_by_lane(log_f), preferred_element_type=F32)
    cum = cum + pltpu.roll(cum, LANES - 1, 1) + pltpu.roll(cum, LANES - 2, 1)
    local = jnp.where(piece_id == 0.0, cum, 0.0)
    goff_ref[0, 0] = carry_ref[...] * LOG2E
    carry_ref[...] = carry_ref[...] + local[tm - 1:tm, :]

    local3 = local + pltpu.roll(local, 1, 1) + pltpu.roll(local, 2, 1)
    aug = pieces_by_lane(local3 * LOG2E).astype(F32)

    fq = proj(C_FQ) * (FOX_HEAD_DIM ** -0.5 * LOG2E)
    fk = proj(C_FK)

    def max_row_norm(v):
        sq = jnp.dot((v * v).astype(BF16), ind_ref[...], preferred_element_type=F32)
        return jnp.sqrt(jnp.max(sq, axis=0, keepdims=True)) * NORM_MARGIN

    norm_ref[0, 0] = jnp.concatenate(
        [max_row_norm(fq), max_row_norm(fk), jnp.zeros((SUBLANES - 2, LANES), F32)], axis=0)
    feat = lane < FOX_HEAD_DIM
    aug_a = jnp.logical_and(lane >= AUG, lane < AUG + N_PIECES)
    aug_b = jnp.logical_and(lane >= AUG + N_PIECES, lane < AUG + 2 * N_PIECES)
    ones_a = jnp.where(aug_a, 1.0, 0.0)
    ones_b = jnp.where(aug_b, 1.0, 0.0)
    for hd in range(HEADS):
        src = slice((hd // 2) * LANES, (hd // 2 + 1) * LANES)
        dst = slice(hd * HEAD_PAD, (hd + 1) * HEAD_PAD)
        q_blk, k_blk = fq[:, src], fk[:, src]
        if hd % 2:
            q_blk = pltpu.roll(q_blk, FOX_HEAD_DIM, 1)
            k_blk = pltpu.roll(k_blk, FOX_HEAD_DIM, 1)
        a_t = pltpu.roll(aug, AUG - N_PIECES * hd, 1)
        b_s = pltpu.roll(-aug, AUG + N_PIECES - N_PIECES * hd, 1)
        qf_ref[0, :, dst] = jnp.where(feat, q_blk, jnp.where(aug_a, a_t, ones_b)).astype(BF16)
        kf_ref[0, :, dst] = jnp.where(feat, k_blk, jnp.where(aug_b, b_s, ones_a)).astype(BF16)
    vft = (_dot_nt(wfvt_ref[...], h) + bfvt_ref[...]).astype(BF16)

    vfill = vfill_ref[...]
    for hd in range(HEADS):
        rows = slice(hd * HEAD_V, (hd + 1) * HEAD_V)
        vmt_ref[0, hd, 0, 0:HEAD_V, :] = vmt[rows, :]
        vmt_ref[0, hd, 0, HEAD_V:, :] = vfill
        vft_ref[0, hd, 0, 0:HEAD_V, :] = vft[rows, :]
        vft_ref[0, hd, 0, HEAD_V:, :] = vfill

    ga_ref[0] = _sigmoid_of_twice(proj(C_GA)).astype(BF16)
    gb_ref[0] = _sigmoid_of_twice(proj(C_GB)).astype(BF16)


def _pre_call(x, pos_row, consts):
    B, S, D = x.shape
    tm = BLK
    nblk = S // tm
    tok = lambda n: pl.BlockSpec((1, tm, n), lambda b, j: (b, j, 0))
    vt_shape = jax.ShapeDtypeStruct((B, HEADS, nblk, VT_ROWS, tm), BF16)
    vt_spec = pl.BlockSpec((1, HEADS, 1, VT_ROWS, tm), lambda b, j: (b, 0, j, 0, 0))
    out_shape = [
        jax.ShapeDtypeStruct((B, S, HEADS * HEAD_PAD), BF16),
        jax.ShapeDtypeStruct((B, S, HEADS * HEAD_PAD), BF16),
        vt_shape,
        jax.ShapeDtypeStruct((B, S, HEADS * HEAD_PAD), BF16),
        jax.ShapeDtypeStruct((B, S, HEADS * HEAD_PAD), BF16),
        vt_shape,
        jax.ShapeDtypeStruct((B, nblk, 1, LANES), F32),
        jax.ShapeDtypeStruct((B, nblk, SUBLANES, LANES), F32),
        jax.ShapeDtypeStruct((B, S, D), BF16),
        jax.ShapeDtypeStruct((B, S, D), BF16),
    ]
    out_specs = [tok(HEADS * HEAD_PAD), tok(HEADS * HEAD_PAD), vt_spec,
                 tok(HEADS * HEAD_PAD), tok(HEADS * HEAD_PAD), vt_spec,
                 pl.BlockSpec((1, 1, 1, LANES), lambda b, j: (b, j, 0, 0)),
                 pl.BlockSpec((1, 1, SUBLANES, LANES), lambda b, j: (b, j, 0, 0)),
                 tok(D), tok(D)]
    in_specs = ([tok(D), pl.BlockSpec((1, 1, tm), lambda b, j: (b, 0, j))]
                + [_const_spec(a.shape) for a in consts])
    return pl.pallas_call(
        _pre_kernel,
        grid=(B, nblk),
        in_specs=in_specs,
        out_specs=out_specs,
        out_shape=out_shape,
        scratch_shapes=[pltpu.VMEM((1, LANES), F32)],
        compiler_params=pltpu.CompilerParams(
            dimension_semantics=("parallel", "arbitrary"), vmem_limit_bytes=VMEM_LIMIT),
        name="pre",
    )(x, pos_row, *consts)


def _attn_kernel(*refs, fox, nblk):
    if fox:
        goff_ref, qn_ref, kn_ref = refs[:3]
        refs = refs[3:]
    q_ref, k_ref, vt_ref, o_ref, s_ref, p_ref, mx_ref, m_ref, a_ref, acc_ref = refs
    t = BLK
    half = t // 2
    g_base = (pl.program_id(0) * N_PAIRS + pl.program_id(1)) * 2

    def stage_q(i, j, diag):
        qoff = pl.multiple_of(i * t, t)
        koff = pl.multiple_of(j * t, t)
        for hd in range(2):
            lanes = slice(hd * HEAD_PAD, (hd + 1) * HEAD_PAD)
            if not diag:
                s = _dot_nt(k_ref[0, pl.ds(koff, t), lanes], q_ref[0, pl.ds(qoff, t), lanes])
                s_ref[hd] = s
                mx_ref[hd] = jnp.max(s, axis=0, keepdims=True)
                continue
            causal = (lax.broadcasted_iota(jnp.int32, (half, half), 0)
                      <= lax.broadcasted_iota(jnp.int32, (half, half), 1))
            k_lo = k_ref[0, pl.ds(koff, half), lanes]
            k_hi = k_ref[0, pl.ds(pl.multiple_of(koff + half, half), half), lanes]
            s_lo = _dot_nt(k_lo, q_ref[0, pl.ds(qoff, t), lanes])
            s_hh = _dot_nt(k_hi, q_ref[0, pl.ds(pl.multiple_of(qoff + half, half), half), lanes])
            s_ll = jnp.where(causal, s_lo[:, :half], -jnp.inf)
            s_hh = jnp.where(causal, s_hh, -jnp.inf)
            s_ref[hd, :half, :half] = s_ll
            s_ref[hd, :half, half:] = s_lo[:, half:]
            s_ref[hd, half:, :half] = jnp.full((half, half), -jnp.inf, F32)
            s_ref[hd, half:, half:] = s_hh
            mx_ref[hd, :, :half] = jnp.max(s_ll, axis=0, keepdims=True)
            mx_ref[hd, :, half:] = jnp.maximum(jnp.max(s_lo[:, half:], axis=0, keepdims=True),
                                               jnp.max(s_hh, axis=0, keepdims=True))

    def stage_e(i, j):
        for hd in range(2):
            m_prev = m_ref[hd]
            if fox:
                delta = goff_ref[g_base + hd, i] - goff_ref[g_base + hd, j]
                m_new = jnp.maximum(m_prev, mx_ref[hd] + delta)
                shift = m_new - delta
            else:
                m_new = jnp.maximum(m_prev, mx_ref[hd])
                shift = m_new
            p_ref[hd] = jnp.exp2(s_ref[hd] - shift).astype(BF16)
            a_ref[hd] = jnp.exp2(m_prev - m_new)
            m_ref[hd] = m_new

    def stage_p(j, diag=False):
        for hd in range(2):
            if not diag:
                acc_ref[hd] = a_ref[hd] * acc_ref[hd] + jnp.dot(
                    vt_ref[0, hd, j], p_ref[hd], preferred_element_type=F32)
                continue
            acc_ref[hd, :, :half] = a_ref[hd, :, :half] * acc_ref[hd, :, :half] + jnp.dot(
                vt_ref[0, hd, j, :, :half], p_ref[hd, :half, :half], preferred_element_type=F32)
            acc_ref[hd, :, half:] = a_ref[hd, :, half:] * acc_ref[hd, :, half:] + jnp.dot(
                vt_ref[0, hd, j], p_ref[hd, :, half:], preferred_element_type=F32)

    def finish(i):
        outs = []
        for hd in range(2):
            acc = acc_ref[hd]
            outs.append(acc[0:HEAD_V, :] / acc[HEAD_V:HEAD_V + 1, :])
        o_ref[0, pl.ds(pl.multiple_of(i * t, t), t), :] = jnp.transpose(
            jnp.concatenate(outs, axis=0)).astype(o_ref.dtype)
        reset()

    def reset():
        m_ref[...] = jnp.full(m_ref.shape, -jnp.inf, F32)
        acc_ref[...] = jnp.zeros_like(acc_ref)

    def step(i, j, nxt, next_diag=False, row_start=False):
        if row_start:
            stage_p(i - 1, diag=True)
            finish(i - 1)
        else:
            stage_p(j - 1)
        stage_e(i, j)
        if nxt is not None:
            stage_q(nxt[0], nxt[1], next_diag)

    def plain_steps(i, j0, count):
        for u in range(count):
            step(i, j0 + u, (i, j0 + u + 1))

    def plain_run(i, j0, n_plain):
        odd = lax.bitwise_and(n_plain, 1) != 0
        triple = jnp.logical_and(odd, n_plain >= 3)
        n_pairs = lax.shift_right_logical(n_plain - jnp.where(triple, 3, 0), 1)

        def pair(_, j):
            plain_steps(i, j, 2)
            return j + 2

        j = lax.fori_loop(0, n_pairs, pair, j0)

        @pl.when(triple)
        def _():
            plain_steps(i, j, 3)

        @pl.when(jnp.logical_and(odd, n_plain < 3))
        def _():
            plain_steps(i, j, 1)

    def first_tile(i):
        if not fox:
            return jnp.int32(0)

        def negligible(j):
            ok = None
            for hd in range(2):
                r = g_base + hd
                bound = (goff_ref[r, i] - goff_ref[r, j + 1]
                         + qn_ref[r, i] * (kn_ref[r, j] + kn_ref[r, i]))
                small = bound < -SKIP_BITS
                ok = small if ok is None else jnp.logical_and(ok, small)
            return ok

        return lax.while_loop(lambda j: jnp.logical_and(j < i - 2, negligible(j)),
                              lambda j: j + 1, jnp.int32(0))

    last = nblk - 1
    reset()
    stage_q(0, 0, True)
    stage_e(0, 0)
    stage_q(1, 0, False)
    step(1, 0, (1, 1), next_diag=True, row_start=True)
    step(1, 1, (2, 0))
    step(2, 0, (2, 1), row_start=True)

    def rest_of_row(i, first):
        plain_run(i, first + 1, i - 2 - first)
        step(i, i - 1, (i, i), next_diag=True)

    def row(i, first):
        rest_of_row(i, first)
        nxt_first = first_tile(i + 1)
        step(i, i, (i + 1, nxt_first))
        step(i + 1, nxt_first, (i + 1, nxt_first + 1), row_start=True)
        return nxt_first

    first = lax.fori_loop(2, last, row, jnp.int32(0))
    rest_of_row(last, first)
    step(last, last, None)
    stage_p(last, diag=True)
    finish(last)


def _attn_call(q, k, vt, tables=None):
    B, S, _ = q.shape
    fox = tables is not None
    t = BLK
    nblk = S // t
    assert nblk >= 3, "the step schedule peels q blocks 0..2"
    in_specs = [
        pl.BlockSpec((1, S, 2 * HEAD_PAD), lambda b, p: (b, 0, p)),
        pl.BlockSpec((1, S, 2 * HEAD_PAD), lambda b, p: (b, 0, p)),
        pl.BlockSpec((1, 2, nblk, VT_ROWS, t), lambda b, p: (b, p, 0, 0, 0)),
    ]
    args = [q, k, vt]
    if fox:
        in_specs = [pl.BlockSpec(memory_space=pltpu.SMEM)] * len(tables) + in_specs
        args = list(tables) + args
    stat = pltpu.VMEM((2, 1, t), F32)
    return pl.pallas_call(
        functools.partial(_attn_kernel, fox=fox, nblk=nblk),
        grid=(B, N_PAIRS),
        in_specs=in_specs,
        out_specs=pl.BlockSpec((1, S, 2 * HEAD_V), lambda b, p: (b, 0, p)),
        out_shape=jax.ShapeDtypeStruct((B, S, HEADS * HEAD_V), BF16),
        scratch_shapes=[pltpu.VMEM((2, t, t), F32), pltpu.VMEM((2, t, t), BF16),
                        stat, stat, stat, pltpu.VMEM((2, VT_ROWS, t), F32)],
        compiler_params=pltpu.CompilerParams(
            dimension_semantics=("parallel", "parallel"),
            vmem_limit_bytes=VMEM_LIMIT),
        name="attn_fox" if fox else "attn_mla",
    )(*args)


def _post_kernel(om_ref, of_ref, ga_ref, gb_ref, x_ref, wom_ref, wof_ref, wout_ref,
                 g1_ref, g2_ref, w1_ref, w2_ref, g3_ref, out_ref):
    y_mla = jnp.dot(om_ref[0], wom_ref[...], preferred_element_type=F32)
    y_fox = jnp.dot(of_ref[0], wof_ref[...], preferred_element_type=F32)
    merged = ga_ref[0].astype(F32) * y_mla + gb_ref[0].astype(F32) * y_fox
    mix = jnp.dot(merged.astype(BF16), wout_ref[...], preferred_element_type=F32)
    x1 = x_ref[0] + _rms(mix, g1_ref[...])
    h2 = _rms(x1, g2_ref[...]).astype(BF16)
    m = jnp.zeros_like(x1)
    for c in range(D_FF // FF_CHUNK):
        sl = slice(c * FF_CHUNK, (c + 1) * FF_CHUNK)
        u = jnp.dot(h2, w1_ref[:, sl], preferred_element_type=F32)
        a = jnp.square(jnp.maximum(u, 0.0)).astype(BF16)
        m = m + jnp.dot(a, w2_ref[sl, :], preferred_element_type=F32)
    out_ref[0] = x1 + _rms(m, g3_ref[...])


def _post_call(o_mla, o_fox, ga, gb, x, wom, wof, wout, g1, g2, w1, w2, g3):
    B, S, D = x.shape
    tm = POST_TM
    tok = lambda n: pl.BlockSpec((1, tm, n), lambda b, j: (b, j, 0))
    in_specs = [tok(512), tok(512), tok(D), tok(D), tok(D)] + [
        _const_spec(a.shape) for a in (wom, wof, wout, g1, g2, w1, w2, g3)]
    return pl.pallas_call(
        _post_kernel,
        grid=(B, S // tm),
        in_specs=in_specs,
        out_specs=tok(D),
        out_shape=jax.ShapeDtypeStruct((B, S, D), F32),
        compiler_params=pltpu.CompilerParams(
            dimension_semantics=("parallel", "parallel"), vmem_limit_bytes=VMEM_LIMIT),
        name="post",
    )(o_mla, o_fox, ga, gb, x, wom, wof, wout, g1, g2, w1, w2, g3)


def _pad_heads(w, width):
    r = w.shape[0]
    w = w.reshape(r, HEADS, width)
    w = jnp.pad(w, ((0, 0), (0, 0), (0, HEAD_PAD - width)))
    return w.reshape(r, HEADS * HEAD_PAD)


def _arrange_in_proj(w, b):
    offs = np.cumsum((0,) + IN_SPLITS)
    cq, ckv, kr, fq, fk, fv, f, ga, gb = [w[:, offs[i]:offs[i + 1]] for i in range(9)]
    bcq, bckv, bkr, bfq, bfk, bfv, bf, bga, bgb = [b[None, offs[i]:offs[i + 1]] for i in range(9)]
    kr_pad = ((0, 0), (MLA_NOPE_DIM, LANES - MLA_NOPE_DIM - MLA_ROPE_DIM))
    f_rep = lambda a: jnp.pad(jnp.repeat(a, N_PIECES, axis=1), ((0, 0), (0, LANES - PIECE_LANES)))
    w_r = jnp.concatenate([cq, ckv, jnp.pad(kr, kr_pad), fq, fk, f_rep(f), 0.5 * ga, 0.5 * gb],
                          axis=1)
    b_r = jnp.concatenate([bcq, bckv, jnp.pad(bkr, kr_pad), bfq, bfk, f_rep(bf),
                           0.5 * bga, 0.5 * bgb], axis=1)
    return w_r.astype(BF16), b_r.astype(F32), fv.T.astype(BF16), bfv.T.astype(F32)


def _static_tables():
    inv = ROPE_THETA ** (-np.arange(HALF_ROPE, dtype=np.float32) / HALF_ROPE)
    freq = inv.reshape(HALF_ROPE, 1)
    piece = np.full((1, LANES), float(N_PIECES), np.float32)
    piece[0, :PIECE_LANES] = np.arange(PIECE_LANES) % N_PIECES
    vfill = np.zeros((VT_ROWS - HEAD_V, BLK), np.float32)
    vfill[0, :] = 1.0
    ind = np.zeros((FOX_HEADS * FOX_HEAD_DIM, LANES), np.float32)
    for hd in range(FOX_HEADS):
        ind[hd * FOX_HEAD_DIM:(hd + 1) * FOX_HEAD_DIM, hd] = 1.0
    return (jnp.asarray(freq), jnp.asarray(piece), jnp.asarray(vfill, BF16),
            jnp.asarray(ind, BF16))


def _layer(x, pos_row, ln_pre_mix, ln_post_mix, ln_pre_mlp, ln_post_mlp, w_in, b_in, q_a_norm,
           w_uq, kv_a_norm, w_uk, w_uv, w_o_mla, w_o_fox, w_out, w_ff1, w_ff2):
    B, S, _ = x.shape
    row = lambda v: v[None, :].astype(F32)
    w_r, b_r, wfvt, bfvt = _arrange_in_proj(w_in, b_in)
    freq, piece, vfill, ind = _static_tables()
    consts = (row(ln_pre_mix), w_r, b_r, row(q_a_norm),
              _pad_heads(w_uq, MLA_NOPE_DIM + MLA_ROPE_DIM).astype(BF16), row(kv_a_norm),
              _pad_heads(w_uk, MLA_NOPE_DIM).astype(BF16), w_uv.T.astype(BF16), wfvt, bfvt,
              freq, piece, vfill, ind)
    (q_mla, k_mla, vt_mla, q_fox, k_fox, vt_fox, goff, norms, ga, gb) = _pre_call(
        x, pos_row, consts)

    per_head = lambda a: jnp.transpose(a, (0, 2, 1)).reshape(B * FOX_HEADS, S // BLK)
    tables = (per_head(goff[:, :, 0, 0:PIECE_LANES:N_PIECES]),
              per_head(norms[:, :, 0, :FOX_HEADS]), per_head(norms[:, :, 1, :FOX_HEADS]))

    o_mla = _attn_call(q_mla, k_mla, vt_mla)
    o_fox = _attn_call(q_fox, k_fox, vt_fox, tables)

    return _post_call(o_mla, o_fox, ga, gb, x, w_o_mla.astype(BF16), w_o_fox.astype(BF16),
                      w_out.astype(BF16), row(ln_post_mix), row(ln_pre_mlp),
                      w_ff1.astype(BF16), w_ff2.astype(BF16), row(ln_post_mlp))


def kernel(x, positions, ln_pre_mix, ln_post_mix, ln_pre_mlp, ln_post_mlp, w_in, b_in,
           q_a_norm, w_uq, kv_a_norm, w_uk, w_uv, w_o_mla, w_o_fox, w_out, w_ff1, w_ff2):
    pos_row = positions[:, None, :]
    for l in range(w_in.shape[0]):
        x = _layer(x, pos_row, ln_pre_mix[l], ln_post_mix[l], ln_pre_mlp[l], ln_post_mlp[l],
                   w_in[l], b_in[l], q_a_norm[l], w_uq[l], kv_a_norm[l], w_uk[l], w_uv[l],
                   w_o_mla[l], w_o_fox[l], w_out[l], w_ff1[l], w_ff2[l])
    return x
```

```python
import functools

import numpy as np
import jax
import jax.numpy as jnp
from jax import lax
from jax.experimental import pallas as pl
from jax.experimental.pallas import tpu as pltpu

F32 = jnp.float32
BF16 = jnp.bfloat16

D_MODEL = 1024
MLA_HEADS = 8
MLA_Q_LORA = 256
MLA_KV_LORA = 128
MLA_NOPE_DIM = 64
MLA_ROPE_DIM = 32
MLA_V_DIM = 64
FOX_HEADS = 8
FOX_HEAD_DIM = 64
D_FF = 4 * D_MODEL
ROPE_THETA = 10000.0
NORM_EPS = 1e-6
IN_SPLITS = (MLA_Q_LORA, MLA_KV_LORA, MLA_ROPE_DIM,
             FOX_HEADS * FOX_HEAD_DIM, FOX_HEADS * FOX_HEAD_DIM, FOX_HEADS * FOX_HEAD_DIM,
             FOX_HEADS, D_MODEL, D_MODEL)

LANES = 128
SUBLANES = 8
HEAD_PAD = 128
HEADS = 8
HEAD_V = 64
VT_ROWS = HEAD_V + 64
HALF_ROPE = MLA_ROPE_DIM // 2
N_PAIRS = HEADS // 2
AUG = 64
N_PIECES = 3
PIECE_LANES = N_PIECES * FOX_HEADS
LOG2E = float(np.log2(np.e))
SKIP_BITS = 160.0
NORM_MARGIN = 1.02
VMEM_LIMIT = 52 * 1024 * 1024

BLK = 512
POST_TM = 512
FF_CHUNK = 512

_W = (MLA_Q_LORA, MLA_KV_LORA, LANES, FOX_HEADS * FOX_HEAD_DIM, FOX_HEADS * FOX_HEAD_DIM, LANES,
      D_MODEL, D_MODEL)
_OFF = tuple(int(v) for v in np.cumsum((0,) + _W))
(C_CQ, C_CKV, C_KR, C_FQ, C_FK, C_F, C_GA, C_GB) = tuple(
    (_OFF[i], _OFF[i + 1]) for i in range(8))


def _rms(x, g):
    return x * lax.rsqrt(jnp.mean(x * x, axis=-1, keepdims=True) + NORM_EPS) * g


def _sigmoid_of_twice(half_x):
    return 0.5 * jnp.tanh(half_x) + 0.5


def _const_spec(shape):
    zeros = (0,) * len(shape)
    return pl.BlockSpec(shape, lambda *_: zeros, pipeline_mode=pl.Buffered(1))


def _dot_nt(a, b):
    return lax.dot_general(a, b, (((1,), (1,)), ((), ())), preferred_element_type=F32)


def _pre_kernel(x_ref, pos_ref, g_ref, w_ref, b_ref, qn_ref, wuq_ref, kvn_ref, wuk_ref,
                wuvt_ref, wfvt_ref, bfvt_ref, freq_ref, piece_ref, vfill_ref, ind_ref,
                qm_ref, km_ref, vmt_ref, qf_ref, kf_ref, vft_ref, goff_ref, norm_ref,
                ga_ref, gb_ref,
                carry_ref):
    tm = x_ref.shape[1]
    lane = lax.broadcasted_iota(jnp.int32, (tm, LANES), 1)

    @pl.when(pl.program_id(1) == 0)
    def _():
        carry_ref[...] = jnp.zeros_like(carry_ref)

    h = _rms(x_ref[0], g_ref[...]).astype(BF16)

    def proj(c):
        return (jnp.dot(h, w_ref[:, c[0]:c[1]], preferred_element_type=F32)
                + b_ref[:, c[0]:c[1]])

    ang = freq_ref[...] * pos_ref[0].astype(F32)
    cos_t = jnp.cos(ang)
    sin_t = jnp.sin(ang)
    below = (MLA_NOPE_DIM, tm)
    above = (LANES - MLA_NOPE_DIM - MLA_ROPE_DIM, tm)
    cos = jnp.transpose(jnp.concatenate(
        [jnp.ones(below, F32), cos_t, cos_t, jnp.ones(above, F32)], axis=0))
    sin = jnp.transpose(jnp.concatenate(
        [jnp.zeros(below, F32), -sin_t, sin_t, jnp.zeros(above, F32)], axis=0))
    s_lo = jnp.where(lane < MLA_NOPE_DIM + HALF_ROPE, sin, 0.0)
    s_hi = jnp.where(lane >= MLA_NOPE_DIM + HALF_ROPE, sin, 0.0)

    def rope(blk, tables):
        c, lo, hi = tables
        return (blk * c + pltpu.roll(blk, LANES - HALF_ROPE, 1) * lo
                + pltpu.roll(blk, HALF_ROPE, 1) * hi)

    c_q = _rms(proj(C_CQ), qn_ref[...]).astype(BF16)
    q_full = jnp.dot(c_q, wuq_ref[...], preferred_element_type=F32)
    q_scale = (MLA_NOPE_DIM + MLA_ROPE_DIM) ** -0.5 * LOG2E
    q_tables = (cos * q_scale, s_lo * q_scale, s_hi * q_scale)
    c_kv = _rms(proj(C_CKV), kvn_ref[...]).astype(BF16)
    k_nope = jnp.dot(c_kv, wuk_ref[...], preferred_element_type=F32)
    k_rot = rope(proj(C_KR), (cos, s_lo, s_hi))
    for hd in range(HEADS):
        sl = slice(hd * HEAD_PAD, (hd + 1) * HEAD_PAD)
        qm_ref[0, :, sl] = rope(q_full[:, sl], q_tables).astype(BF16)
        km_ref[0, :, sl] = (k_nope[:, sl] + k_rot).astype(BF16)
    vmt = _dot_nt(wuvt_ref[...], c_kv).astype(BF16)

    piece_id = piece_ref[...]

    def pieces_by_lane(v):
        p0 = v.astype(BF16)
        r1 = v - p0.astype(F32)
        p1 = r1.astype(BF16)
        p2 = (r1 - p1.astype(F32)).astype(BF16)
        return jnp.where(piece_id == 0.0, p0, jnp.where(piece_id == 1.0, p1, p2))

    f_logit = proj(C_F)
    log_f = jnp.minimum(f_logit, 0.0) - jnp.log1p(jnp.exp(-jnp.abs(f_logit)))
    row = lax.broadcasted_iota(jnp.int32, (tm, tm), 0)
    col = lax.broadcasted_iota(jnp.int32, (tm, tm), 1)
    tri = (row >= col).astype(BF16)
    cum = jnp.dot(tri, pieces_by_lane(log_f), preferred_element_type=F32)
    cum = cum + pltpu.roll(cum, LANES - 1, 1) + pltpu.roll(cum, LANES - 2, 1)
    local = jnp.where(piece_id == 0.0, cum, 0.0)
    goff_ref[0, 0] = carry_ref[...] * LOG2E
    carry_ref[...] = carry_ref[...] + local[tm - 1:tm, :]

    local3 = local + pltpu.roll(local, 1, 1) + pltpu.roll(local, 2, 1)
    aug = pieces_by_lane(local3 * LOG2E).astype(F32)

    fq = proj(C_FQ) * (FOX_HEAD_DIM ** -0.5 * LOG2E)
    fk = proj(C_FK)

    def max_row_norm(v):
        sq = jnp.dot((v * v).astype(BF16), ind_ref[...], preferred_element_type=F32)
        return jnp.sqrt(jnp.max(sq, axis=0, keepdims=True)) * NORM_MARGIN

    norm_ref[0, 0] = jnp.concatenate(
        [max_row_norm(fq), max_row_norm(fk), jnp.zeros((SUBLANES - 2, LANES), F32)], axis=0)
    feat = lane < FOX_HEAD_DIM
    aug_a = jnp.logical_and(lane >= AUG, lane < AUG + N_PIECES)
    aug_b = jnp.logical_and(lane >= AUG + N_PIECES, lane < AUG + 2 * N_PIECES)
    ones_a = jnp.where(aug_a, 1.0, 0.0)
    ones_b = jnp.where(aug_b, 1.0, 0.0)
    for hd in range(HEADS):
        src = slice((hd // 2) * LANES, (hd // 2 + 1) * LANES)
        dst = slice(hd * HEAD_PAD, (hd + 1) * HEAD_PAD)
        q_blk, k_blk = fq[:, src], fk[:, src]
        if hd % 2:
            q_blk = pltpu.roll(q_blk, FOX_HEAD_DIM, 1)
            k_blk = pltpu.roll(k_blk, FOX_HEAD_DIM, 1)
        a_t = pltpu.roll(aug, AUG - N_PIECES * hd, 1)
        b_s = pltpu.roll(-aug, AUG + N_PIECES - N_PIECES * hd, 1)
        qf_ref[0, :, dst] = jnp.where(feat, q_blk, jnp.where(aug_a, a_t, ones_b)).astype(BF16)
        kf_ref[0, :, dst] = jnp.where(feat, k_blk, jnp.where(aug_b, b_s, ones_a)).astype(BF16)
    vft = (_dot_nt(wfvt_ref[...], h) + bfvt_ref[...]).astype(BF16)

    vfill = vfill_ref[...]
    for hd in range(HEADS):
        rows = slice(hd * HEAD_V, (hd + 1) * HEAD_V)
        vmt_ref[0, hd, 0, 0:HEAD_V, :] = vmt[rows, :]
        vmt_ref[0, hd, 0, HEAD_V:, :] = vfill
        vft_ref[0, hd, 0, 0:HEAD_V, :] = vft[rows, :]
        vft_ref[0, hd, 0, HEAD_V:, :] = vfill

    ga_ref[0] = _sigmoid_of_twice(proj(C_GA)).astype(BF16)
    gb_ref[0] = _sigmoid_of_twice(proj(C_GB)).astype(BF16)


def _pre_call(x, pos_row, consts):
    B, S, D = x.shape
    tm = BLK
    nblk = S // tm
    tok = lambda n: pl.BlockSpec((1, tm, n), lambda b, j: (b, j, 0))
    vt_shape = jax.ShapeDtypeStruct((B, HEADS, nblk, VT_ROWS, tm), BF16)
    vt_spec = pl.BlockSpec((1, HEADS, 1, VT_ROWS, tm), lambda b, j: (b, 0, j, 0, 0))
    out_shape = [
        jax.ShapeDtypeStruct((B, S, HEADS * HEAD_PAD), BF16),
        jax.ShapeDtypeStruct((B, S, HEADS * HEAD_PAD), BF16),
        vt_shape,
        jax.ShapeDtypeStruct((B, S, HEADS * HEAD_PAD), BF16),
        jax.ShapeDtypeStruct((B, S, HEADS * HEAD_PAD), BF16),
        vt_shape,
        jax.ShapeDtypeStruct((B, nblk, 1, LANES), F32),
        jax.ShapeDtypeStruct((B, nblk, SUBLANES, LANES), F32),
        jax.ShapeDtypeStruct((B, S, D), BF16),
        jax.ShapeDtypeStruct((B, S, D), BF16),
    ]
    out_specs = [tok(HEADS * HEAD_PAD), tok(HEADS * HEAD_PAD), vt_spec,
                 tok(HEADS * HEAD_PAD), tok(HEADS * HEAD_PAD), vt_spec,
                 pl.BlockSpec((1, 1, 1, LANES), lambda b, j: (b, j, 0, 0)),
                 pl.BlockSpec((1, 1, SUBLANES, LANES), lambda b, j: (b, j, 0, 0)),
                 tok(D), tok(D)]
    in_specs = ([tok(D), pl.BlockSpec((1, 1, tm), lambda b, j: (b, 0, j))]
                + [_const_spec(a.shape) for a in consts])
    return pl.pallas_call(
        _pre_kernel,
        grid=(B, nblk),
        in_specs=in_specs,
        out_specs=out_specs,
        out_shape=out_shape,
        scratch_shapes=[pltpu.VMEM((1, LANES), F32)],
        compiler_params=pltpu.CompilerParams(
            dimension_semantics=("parallel", "arbitrary"), vmem_limit_bytes=VMEM_LIMIT),
        name="pre",
    )(x, pos_row, *consts)


def _attn_kernel(*refs, fox, nblk):
    if fox:
        goff_ref, qn_ref, kn_ref = refs[:3]
        refs = refs[3:]
    q_ref, k_ref, vt_ref, o_ref, s_ref, p_ref, mx_ref, m_ref, a_ref, acc_ref = refs
    t = BLK
    half = t // 2
    g_base = (pl.program_id(0) * N_PAIRS + pl.program_id(1)) * 2

    def stage_q(i, j, diag):
        qoff = pl.multiple_of(i * t, t)
        koff = pl.multiple_of(j * t, t)
        for hd in range(2):
            lanes = slice(hd * HEAD_PAD, (hd + 1) * HEAD_PAD)
            if not diag:
                s = _dot_nt(k_ref[0, pl.ds(koff, t), lanes], q_ref[0, pl.ds(qoff, t), lanes])
                s_ref[hd] = s
                mx_ref[hd] = jnp.max(s, axis=0, keepdims=True)
                continue
            causal = (lax.broadcasted_iota(jnp.int32, (half, half), 0)
                      <= lax.broadcasted_iota(jnp.int32, (half, half), 1))
            k_lo = k_ref[0, pl.ds(koff, half), lanes]
            k_hi = k_ref[0, pl.ds(pl.multiple_of(koff + half, half), half), lanes]
            s_lo = _dot_nt(k_lo, q_ref[0, pl.ds(qoff, t), lanes])
            s_hh = _dot_nt(k_hi, q_ref[0, pl.ds(pl.multiple_of(qoff + half, half), half), lanes])
            s_ll = jnp.where(causal, s_lo[:, :half], -jnp.inf)
            s_hh = jnp.where(causal, s_hh, -jnp.inf)
            s_ref[hd, :half, :half] = s_ll
            s_ref[hd, :half, half:] = s_lo[:, half:]
            s_ref[hd, half:, :half] = jnp.full((half, half), -jnp.inf, F32)
            s_ref[hd, half:, half:] = s_hh
            mx_ref[hd, :, :half] = jnp.max(s_ll, axis=0, keepdims=True)
            mx_ref[hd, :, half:] = jnp.maximum(jnp.max(s_lo[:, half:], axis=0, keepdims=True),
                                               jnp.max(s_hh, axis=0, keepdims=True))

    def stage_e(i, j):
        for hd in range(2):
            m_prev = m_ref[hd]
            if fox:
                delta = goff_ref[g_base + hd, i] - goff_ref[g_base + hd, j]
                m_new = jnp.maximum(m_prev, mx_ref[hd] + delta)
                shift = m_new - delta
            else:
                m_new = jnp.maximum(m_prev, mx_ref[hd])
                shift = m_new
            p_ref[hd] = jnp.exp2(s_ref[hd] - shift).astype(BF16)
            a_ref[hd] = jnp.exp2(m_prev - m_new)
            m_ref[hd] = m_new

    def stage_p(j, diag=False):
        for hd in range(2):
            if not diag:
                acc_ref[hd] = a_ref[hd] * acc_ref[hd] + jnp.dot(
                    vt_ref[0, hd, j], p_ref[hd], preferred_element_type=F32)
                continue
            acc_ref[hd, :, :half] = a_ref[hd, :, :half] * acc_ref[hd, :, :half] + jnp.dot(
                vt_ref[0, hd, j, :, :half], p_ref[hd, :half, :half], preferred_element_type=F32)
            acc_ref[hd, :, half:] = a_ref[hd, :, half:] * acc_ref[hd, :, half:] + jnp.dot(
                vt_ref[0, hd, j], p_ref[hd, :, half:], preferred_element_type=F32)

    def finish(i):
        outs = []
        for hd in range(2):
            acc = acc_ref[hd]
            outs.append(acc[0:HEAD_V, :] / acc[HEAD_V:HEAD_V + 1, :])
        o_ref[0, pl.ds(pl.multiple_of(i * t, t), t), :] = jnp.transpose(
            jnp.concatenate(outs, axis=0)).astype(o_ref.dtype)
        reset()

    def reset():
        m_ref[...] = jnp.full(m_ref.shape, -jnp.inf, F32)
        acc_ref[...] = jnp.zeros_like(acc_ref)

    def step(i, j, nxt, next_diag=False, row_start=False):
        if row_start:
            stage_p(i - 1, diag=True)
            finish(i - 1)
        else:
            stage_p(j - 1)
        stage_e(i, j)
        if nxt is not None:
            stage_q(nxt[0], nxt[1], next_diag)

    def plain_steps(i, j0, count):
        for u in range(count):
            step(i, j0 + u, (i, j0 + u + 1))

    def plain_run(i, j0, n_plain):
        def pair(_, j):
            plain_steps(i, j, 2)
            return j + 2

        j = lax.fori_loop(0, lax.shift_right_logical(n_plain, 1), pair, j0)

        @pl.when(lax.bitwise_and(n_plain, 1) != 0)
        def _():
            plain_steps(i, j, 1)

    def first_tile(i):
        if not fox:
            return jnp.int32(0)

        def negligible(j):
            ok = None
            for hd in range(2):
                r = g_base + hd
                bound = (goff_ref[r, i] - goff_ref[r, j + 1]
                         + qn_ref[r, i] * (kn_ref[r, j] + kn_ref[r, i]))
                small = bound < -SKIP_BITS
                ok = small if ok is None else jnp.logical_and(ok, small)
            return ok

        return lax.while_loop(lambda j: jnp.logical_and(j < i - 2, negligible(j)),
                              lambda j: j + 1, jnp.int32(0))

    last = nblk - 1
    reset()
    stage_q(0, 0, True)
    stage_e(0, 0)
    stage_q(1, 0, False)
    step(1, 0, (1, 1), next_diag=True, row_start=True)
    step(1, 1, (2, 0))
    step(2, 0, (2, 1), row_start=True)

    def rest_of_row(i, first):
        plain_run(i, first + 1, i - 2 - first)
        step(i, i - 1, (i, i), next_diag=True)

    def row(i, first):
        rest_of_row(i, first)
        nxt_first = first_tile(i + 1)
        step(i, i, (i + 1, nxt_first))
        step(i + 1, nxt_first, (i + 1, nxt_first + 1), row_start=True)
        return nxt_first

    first = lax.fori_loop(2, last, row, jnp.int32(0))
    rest_of_row(last, first)
    step(last, last, None)
    stage_p(last, diag=True)
    finish(last)


def _attn_call(q, k, vt, tables=None):
    B, S, _ = q.shape
    fox = tables is not None
    t = BLK
    nblk = S // t
    assert nblk >= 3, "the step schedule peels q blocks 0..2"
    in_specs = [
        pl.BlockSpec((1, S, 2 * HEAD_PAD), lambda b, p: (b, 0, p)),
        pl.BlockSpec((1, S, 2 * HEAD_PAD), lambda b, p: (b, 0, p)),
        pl.BlockSpec((1, 2, nblk, VT_ROWS, t), lambda b, p: (b, p, 0, 0, 0)),
    ]
    args = [q, k, vt]
    if fox:
        in_specs = [pl.BlockSpec(memory_space=pltpu.SMEM)] * len(tables) + in_specs
        args = list(tables) + args
    stat = pltpu.VMEM((2, 1, t), F32)
    return pl.pallas_call(
        functools.partial(_attn_kernel, fox=fox, nblk=nblk),
        grid=(B, N_PAIRS),
        in_specs=in_specs,
        out_specs=pl.BlockSpec((1, S, 2 * HEAD_V), lambda b, p: (b, 0, p)),
        out_shape=jax.ShapeDtypeStruct((B, S, HEADS * HEAD_V), BF16),
        scratch_shapes=[pltpu.VMEM((2, t, t), F32), pltpu.VMEM((2, t, t), BF16),
                        stat, stat, stat, pltpu.VMEM((2, VT_ROWS, t), F32)],
        compiler_params=pltpu.CompilerParams(
            dimension_semantics=("parallel", "parallel"),
            vmem_limit_bytes=VMEM_LIMIT),
        name="attn_fox" if fox else "attn_mla",
    )(*args)


def _post_kernel(om_ref, of_ref, ga_ref, gb_ref, x_ref, wom_ref, wof_ref, wout_ref,
                 g1_ref, g2_ref, w1_ref, w2_ref, g3_ref, out_ref):
    y_mla = jnp.dot(om_ref[0], wom_ref[...], preferred_element_type=F32)
    y_fox = jnp.dot(of_ref[0], wof_ref[...], preferred_element_type=F32)
    merged = ga_ref[0].astype(F32) * y_mla + gb_ref[0].astype(F32) * y_fox
    mix = jnp.dot(merged.astype(BF16), wout_ref[...], preferred_element_type=F32)
    x1 = x_ref[0] + _rms(mix, g1_ref[...])
    h2 = _rms(x1, g2_ref[...]).astype(BF16)
    m = jnp.zeros_like(x1)
    for c in range(D_FF // FF_CHUNK):
        sl = slice(c * FF_CHUNK, (c + 1) * FF_CHUNK)
        u = jnp.dot(h2, w1_ref[:, sl], preferred_element_type=F32)
        a = jnp.square(jnp.maximum(u, 0.0)).astype(BF16)
        m = m + jnp.dot(a, w2_ref[sl, :], preferred_element_type=F32)
    out_ref[0] = x1 + _rms(m, g3_ref[...])


def _post_call(o_mla, o_fox, ga, gb, x, wom, wof, wout, g1, g2, w1, w2, g3):
    B, S, D = x.shape
    tm = POST_TM
    tok = lambda n: pl.BlockSpec((1, tm, n), lambda b, j: (b, j, 0))
    in_specs = [tok(512), tok(512), tok(D), tok(D), tok(D)] + [
        _const_spec(a.shape) for a in (wom, wof, wout, g1, g2, w1, w2, g3)]
    return pl.pallas_call(
        _post_kernel,
        grid=(B, S // tm),
        in_specs=in_specs,
        out_specs=tok(D),
        out_shape=jax.ShapeDtypeStruct((B, S, D), F32),
        compiler_params=pltpu.CompilerParams(
            dimension_semantics=("parallel", "parallel"), vmem_limit_bytes=VMEM_LIMIT),
        name="post",
    )(o_mla, o_fox, ga, gb, x, wom, wof, wout, g1, g2, w1, w2, g3)


def _pad_heads(w, width):
    r = w.shape[0]
    w = w.reshape(r, HEADS, width)
    w = jnp.pad(w, ((0, 0), (0, 0), (0, HEAD_PAD - width)))
    return w.reshape(r, HEADS * HEAD_PAD)


def _arrange_in_proj(w, b):
    offs = np.cumsum((0,) + IN_SPLITS)
    cq, ckv, kr, fq, fk, fv, f, ga, gb = [w[:, offs[i]:offs[i + 1]] for i in range(9)]
    bcq, bckv, bkr, bfq, bfk, bfv, bf, bga, bgb = [b[None, offs[i]:offs[i + 1]] for i in range(9)]
    kr_pad = ((0, 0), (MLA_NOPE_DIM, LANES - MLA_NOPE_DIM - MLA_ROPE_DIM))
    f_rep = lambda a: jnp.pad(jnp.repeat(a, N_PIECES, axis=1), ((0, 0), (0, LANES - PIECE_LANES)))
    w_r = jnp.concatenate([cq, ckv, jnp.pad(kr, kr_pad), fq, fk, f_rep(f), 0.5 * ga, 0.5 * gb],
                          axis=1)
    b_r = jnp.concatenate([bcq, bckv, jnp.pad(bkr, kr_pad), bfq, bfk, f_rep(bf),
                           0.5 * bga, 0.5 * bgb], axis=1)
    return w_r.astype(BF16), b_r.astype(F32), fv.T.astype(BF16), bfv.T.astype(F32)


def _static_tables():
    inv = ROPE_THETA ** (-np.arange(HALF_ROPE, dtype=np.float32) / HALF_ROPE)
    freq = inv.reshape(HALF_ROPE, 1)
    piece = np.full((1, LANES), float(N_PIECES), np.float32)
    piece[0, :PIECE_LANES] = np.arange(PIECE_LANES) % N_PIECES
    vfill = np.zeros((VT_ROWS - HEAD_V, BLK), np.float32)
    vfill[0, :] = 1.0
    ind = np.zeros((FOX_HEADS * FOX_HEAD_DIM, LANES), np.float32)
    for hd in range(FOX_HEADS):
        ind[hd * FOX_HEAD_DIM:(hd + 1) * FOX_HEAD_DIM, hd] = 1.0
    return (jnp.asarray(freq), jnp.asarray(piece), jnp.asarray(vfill, BF16),
            jnp.asarray(ind, BF16))


def _layer(x, pos_row, ln_pre_mix, ln_post_mix, ln_pre_mlp, ln_post_mlp, w_in, b_in, q_a_norm,
           w_uq, kv_a_norm, w_uk, w_uv, w_o_mla, w_o_fox, w_out, w_ff1, w_ff2):
    B, S, _ = x.shape
    row = lambda v: v[None, :].astype(F32)
    w_r, b_r, wfvt, bfvt = _arrange_in_proj(w_in, b_in)
    freq, piece, vfill, ind = _static_tables()
    consts = (row(ln_pre_mix), w_r, b_r, row(q_a_norm),
              _pad_heads(w_uq, MLA_NOPE_DIM + MLA_ROPE_DIM).astype(BF16), row(kv_a_norm),
              _pad_heads(w_uk, MLA_NOPE_DIM).astype(BF16), w_uv.T.astype(BF16), wfvt, bfvt,
              freq, piece, vfill, ind)
    (q_mla, k_mla, vt_mla, q_fox, k_fox, vt_fox, goff, norms, ga, gb) = _pre_call(
        x, pos_row, consts)

    per_head = lambda a: jnp.transpose(a, (0, 2, 1)).reshape(B * FOX_HEADS, S // BLK)
    tables = (per_head(goff[:, :, 0, 0:PIECE_LANES:N_PIECES]),
              per_head(norms[:, :, 0, :FOX_HEADS]), per_head(norms[:, :, 1, :FOX_HEADS]))

    o_mla = _attn_call(q_mla, k_mla, vt_mla)
    o_fox = _attn_call(q_fox, k_fox, vt_fox, tables)

    return _post_call(o_mla, o_fox, ga, gb, x, w_o_mla.astype(BF16), w_o_fox.astype(BF16),
                      w_out.astype(BF16), row(ln_post_mix), row(ln_pre_mlp),
                      w_ff1.astype(BF16), w_ff2.astype(BF16), row(ln_post_mlp))


def kernel(x, positions, ln_pre_mix, ln_post_mix, ln_pre_mlp, ln_post_mlp, w_in, b_in,
           q_a_norm, w_uq, kv_a_norm, w_uk, w_uv, w_o_mla, w_o_fox, w_out, w_ff1, w_ff2):
    pos_row = positions[:, None, :]
    for l in range(w_in.shape[0]):
        x = _layer(x, pos_row, ln_pre_mix[l], ln_post_mix[l], ln_pre_mlp[l], ln_post_mlp[l],
                   w_in[l], b_in[l], q_a_norm[l], w_uq[l], kv_a_norm[l], w_uk[l], w_uv[l],
                   w_o_mla[l], w_o_fox[l], w_out[l], w_ff1[l], w_ff2[l])
    return x
```

```python
import functools

import numpy as np
import jax
import jax.numpy as jnp
from jax import lax
from jax.experimental import pallas as pl
from jax.experimental.pallas import tpu as pltpu

F32 = jnp.float32
BF16 = jnp.bfloat16

D_MODEL = 1024
MLA_HEADS = 8
MLA_Q_LORA = 256
MLA_KV_LORA = 128
MLA_NOPE_DIM = 64
MLA_ROPE_DIM = 32
MLA_V_DIM = 64
FOX_HEADS = 8
FOX_HEAD_DIM = 64
D_FF = 4 * D_MODEL
ROPE_THETA = 10000.0
NORM_EPS = 1e-6
IN_SPLITS = (MLA_Q_LORA, MLA_KV_LORA, MLA_ROPE_DIM,
             FOX_HEADS * FOX_HEAD_DIM, FOX_HEADS * FOX_HEAD_DIM, FOX_HEADS * FOX_HEAD_DIM,
             FOX_HEADS, D_MODEL, D_MODEL)

LANES = 128
SUBLANES = 8
HEAD_PAD = 128
HEADS = 8
HEAD_V = 64
VT_ROWS = HEAD_V + 64
ACC_ROWS = HEAD_V + 8
HALF_ROPE = MLA_ROPE_DIM // 2
N_PAIRS = HEADS // 2
AUG = 64
N_PIECES = 3
PIECE_LANES = N_PIECES * FOX_HEADS
LOG2E = float(np.log2(np.e))
SKIP_BITS = 160.0
NORM_MARGIN = 1.02
VMEM_LIMIT = 52 * 1024 * 1024

BLK = 512
POST_TM = 512
FF_CHUNK = 512

_W = (MLA_Q_LORA, MLA_KV_LORA, LANES, FOX_HEADS * FOX_HEAD_DIM, FOX_HEADS * FOX_HEAD_DIM, LANES,
      D_MODEL, D_MODEL)
_OFF = tuple(int(v) for v in np.cumsum((0,) + _W))
(C_CQ, C_CKV, C_KR, C_FQ, C_FK, C_F, C_GA, C_GB) = tuple(
    (_OFF[i], _OFF[i + 1]) for i in range(8))


def _rms(x, g):
    return x * lax.rsqrt(jnp.mean(x * x, axis=-1, keepdims=True) + NORM_EPS) * g


def _sigmoid_of_twice(half_x):
    return 0.5 * jnp.tanh(half_x) + 0.5


def _const_spec(shape):
    zeros = (0,) * len(shape)
    return pl.BlockSpec(shape, lambda *_: zeros, pipeline_mode=pl.Buffered(1))


def _dot_nt(a, b):
    return lax.dot_general(a, b, (((1,), (1,)), ((), ())), preferred_element_type=F32)


def _pre_kernel(x_ref, pos_ref, g_ref, w_ref, b_ref, qn_ref, wuq_ref, kvn_ref, wuk_ref,
                wuvt_ref, wfvt_ref, bfvt_ref, freq_ref, piece_ref, vfill_ref, ind_ref,
                qm_ref, km_ref, vmt_ref, qf_ref, kf_ref, vft_ref, goff_ref, norm_ref,
                ga_ref, gb_ref,
                carry_ref):
    tm = x_ref.shape[1]
    lane = lax.broadcasted_iota(jnp.int32, (tm, LANES), 1)

    @pl.when(pl.program_id(1) == 0)
    def _():
        carry_ref[...] = jnp.zeros_like(carry_ref)

    h = _rms(x_ref[0], g_ref[...]).astype(BF16)

    def proj(c):
        return (jnp.dot(h, w_ref[:, c[0]:c[1]], preferred_element_type=F32)
                + b_ref[:, c[0]:c[1]])

    ang = freq_ref[...] * pos_ref[0].astype(F32)
    cos_t = jnp.cos(ang)
    sin_t = jnp.sin(ang)
    below = (MLA_NOPE_DIM, tm)
    above = (LANES - MLA_NOPE_DIM - MLA_ROPE_DIM, tm)
    cos = jnp.transpose(jnp.concatenate(
        [jnp.ones(below, F32), cos_t, cos_t, jnp.ones(above, F32)], axis=0))
    sin = jnp.transpose(jnp.concatenate(
        [jnp.zeros(below, F32), -sin_t, sin_t, jnp.zeros(above, F32)], axis=0))
    s_lo = jnp.where(lane < MLA_NOPE_DIM + HALF_ROPE, sin, 0.0)
    s_hi = jnp.where(lane >= MLA_NOPE_DIM + HALF_ROPE, sin, 0.0)

    def rope(blk, tables):
        c, lo, hi = tables
        return (blk * c + pltpu.roll(blk, LANES - HALF_ROPE, 1) * lo
                + pltpu.roll(blk, HALF_ROPE, 1) * hi)

    c_q = _rms(proj(C_CQ), qn_ref[...]).astype(BF16)
    q_full = jnp.dot(c_q, wuq_ref[...], preferred_element_type=F32)
    q_scale = (MLA_NOPE_DIM + MLA_ROPE_DIM) ** -0.5 * LOG2E
    q_tables = (cos * q_scale, s_lo * q_scale, s_hi * q_scale)
    c_kv = _rms(proj(C_CKV), kvn_ref[...]).astype(BF16)
    k_nope = jnp.dot(c_kv, wuk_ref[...], preferred_element_type=F32)
    k_rot = rope(proj(C_KR), (cos, s_lo, s_hi))
    for hd in range(HEADS):
        sl = slice(hd * HEAD_PAD, (hd + 1) * HEAD_PAD)
        qm_ref[0, :, sl] = rope(q_full[:, sl], q_tables).astype(BF16)
        km_ref[0, :, sl] = (k_nope[:, sl] + k_rot).astype(BF16)
    vmt = _dot_nt(wuvt_ref[...], c_kv).astype(BF16)

    piece_id = piece_ref[...]

    def pieces_by_lane(v):
        p0 = v.astype(BF16)
        r1 = v - p0.astype(F32)
        p1 = r1.astype(BF16)
        p2 = (r1 - p1.astype(F32)).astype(BF16)
        return jnp.where(piece_id == 0.0, p0, jnp.where(piece_id == 1.0, p1, p2))

    f_logit = proj(C_F)
    log_f = jnp.minimum(f_logit, 0.0) - jnp.log1p(jnp.exp(-jnp.abs(f_logit)))
    row = lax.broadcasted_iota(jnp.int32, (tm, tm), 0)
    col = lax.broadcasted_iota(jnp.int32, (tm, tm), 1)
    tri = (row >= col).astype(BF16)
    cum = jnp.dot(tri, pieces_by_lane(log_f), preferred_element_type=F32)
    cum = cum + pltpu.roll(cum, LANES - 1, 1) + pltpu.roll(cum, LANES - 2, 1)
    local = jnp.where(piece_id == 0.0, cum, 0.0)
    goff_ref[0, 0] = carry_ref[...] * LOG2E
    carry_ref[...] = carry_ref[...] + local[tm - 1:tm, :]

    local3 = local + pltpu.roll(local, 1, 1) + pltpu.roll(local, 2, 1)
    aug = pieces_by_lane(local3 * LOG2E).astype(F32)

    fq = proj(C_FQ) * (FOX_HEAD_DIM ** -0.5 * LOG2E)
    fk = proj(C_FK)

    def max_row_norm(v):
        sq = jnp.dot((v * v).astype(BF16), ind_ref[...], preferred_element_type=F32)
        return jnp.sqrt(jnp.max(sq, axis=0, keepdims=True)) * NORM_MARGIN

    norm_ref[0, 0] = jnp.concatenate(
        [max_row_norm(fq), max_row_norm(fk), jnp.zeros((SUBLANES - 2, LANES), F32)], axis=0)
    feat = lane < FOX_HEAD_DIM
    aug_a = jnp.logical_and(lane >= AUG, lane < AUG + N_PIECES)
    aug_b = jnp.logical_and(lane >= AUG + N_PIECES, lane < AUG + 2 * N_PIECES)
    ones_a = jnp.where(aug_a, 1.0, 0.0)
    ones_b = jnp.where(aug_b, 1.0, 0.0)
    for hd in range(HEADS):
        src = slice((hd // 2) * LANES, (hd // 2 + 1) * LANES)
        dst = slice(hd * HEAD_PAD, (hd + 1) * HEAD_PAD)
        q_blk, k_blk = fq[:, src], fk[:, src]
        if hd % 2:
            q_blk = pltpu.roll(q_blk, FOX_HEAD_DIM, 1)
            k_blk = pltpu.roll(k_blk, FOX_HEAD_DIM, 1)
        a_t = pltpu.roll(aug, AUG - N_PIECES * hd, 1)
        b_s = pltpu.roll(-aug, AUG + N_PIECES - N_PIECES * hd, 1)
        qf_ref[0, :, dst] = jnp.where(feat, q_blk, jnp.where(aug_a, a_t, ones_b)).astype(BF16)
        kf_ref[0, :, dst] = jnp.where(feat, k_blk, jnp.where(aug_b, b_s, ones_a)).astype(BF16)
    vft = (_dot_nt(wfvt_ref[...], h) + bfvt_ref[...]).astype(BF16)

    vfill = vfill_ref[...]
    for hd in range(HEADS):
        rows = slice(hd * HEAD_V, (hd + 1) * HEAD_V)
        vmt_ref[0, hd, 0, 0:HEAD_V, :] = vmt[rows, :]
        vmt_ref[0, hd, 0, HEAD_V:, :] = vfill
        vft_ref[0, hd, 0, 0:HEAD_V, :] = vft[rows, :]
        vft_ref[0, hd, 0, HEAD_V:, :] = vfill

    ga_ref[0] = _sigmoid_of_twice(proj(C_GA)).astype(BF16)
    gb_ref[0] = _sigmoid_of_twice(proj(C_GB)).astype(BF16)


def _pre_call(x, pos_row, consts):
    B, S, D = x.shape
    tm = BLK
    nblk = S // tm
    tok = lambda n: pl.BlockSpec((1, tm, n), lambda b, j: (b, j, 0))
    vt_shape = jax.ShapeDtypeStruct((B, HEADS, nblk, VT_ROWS, tm), BF16)
    vt_spec = pl.BlockSpec((1, HEADS, 1, VT_ROWS, tm), lambda b, j: (b, 0, j, 0, 0))
    out_shape = [
        jax.ShapeDtypeStruct((B, S, HEADS * HEAD_PAD), BF16),
        jax.ShapeDtypeStruct((B, S, HEADS * HEAD_PAD), BF16),
        vt_shape,
        jax.ShapeDtypeStruct((B, S, HEADS * HEAD_PAD), BF16),
        jax.ShapeDtypeStruct((B, S, HEADS * HEAD_PAD), BF16),
        vt_shape,
        jax.ShapeDtypeStruct((B, nblk, 1, LANES), F32),
        jax.ShapeDtypeStruct((B, nblk, SUBLANES, LANES), F32),
        jax.ShapeDtypeStruct((B, S, D), BF16),
        jax.ShapeDtypeStruct((B, S, D), BF16),
    ]
    out_specs = [tok(HEADS * HEAD_PAD), tok(HEADS * HEAD_PAD), vt_spec,
                 tok(HEADS * HEAD_PAD), tok(HEADS * HEAD_PAD), vt_spec,
                 pl.BlockSpec((1, 1, 1, LANES), lambda b, j: (b, j, 0, 0)),
                 pl.BlockSpec((1, 1, SUBLANES, LANES), lambda b, j: (b, j, 0, 0)),
                 tok(D), tok(D)]
    in_specs = ([tok(D), pl.BlockSpec((1, 1, tm), lambda b, j: (b, 0, j))]
                + [_const_spec(a.shape) for a in consts])
    return pl.pallas_call(
        _pre_kernel,
        grid=(B, nblk),
        in_specs=in_specs,
        out_specs=out_specs,
        out_shape=out_shape,
        scratch_shapes=[pltpu.VMEM((1, LANES), F32)],
        compiler_params=pltpu.CompilerParams(
            dimension_semantics=("parallel", "arbitrary"), vmem_limit_bytes=VMEM_LIMIT),
        name="pre",
    )(x, pos_row, *consts)


def _attn_kernel(*refs, fox, nblk):
    if fox:
        goff_ref, qn_ref, kn_ref = refs[:3]
        refs = refs[3:]
    q_ref, k_ref, vt_ref, o_ref, s_ref, p_ref, mx_ref, m_ref, a_ref, acc_ref = refs
    t = BLK
    half = t // 2
    g_base = (pl.program_id(0) * N_PAIRS + pl.program_id(1)) * 2

    def stage_q(i, j, diag):
        qoff = pl.multiple_of(i * t, t)
        koff = pl.multiple_of(j * t, t)
        for hd in range(2):
            lanes = slice(hd * HEAD_PAD, (hd + 1) * HEAD_PAD)
            if not diag:
                s = _dot_nt(k_ref[0, pl.ds(koff, t), lanes], q_ref[0, pl.ds(qoff, t), lanes])
                s_ref[hd] = s
                mx_ref[hd] = jnp.max(s, axis=0, keepdims=True)
                continue
            causal = (lax.broadcasted_iota(jnp.int32, (half, half), 0)
                      <= lax.broadcasted_iota(jnp.int32, (half, half), 1))
            k_lo = k_ref[0, pl.ds(koff, half), lanes]
            k_hi = k_ref[0, pl.ds(pl.multiple_of(koff + half, half), half), lanes]
            s_lo = _dot_nt(k_lo, q_ref[0, pl.ds(qoff, t), lanes])
            s_hh = _dot_nt(k_hi, q_ref[0, pl.ds(pl.multiple_of(qoff + half, half), half), lanes])
            s_ll = jnp.where(causal, s_lo[:, :half], -jnp.inf)
            s_hh = jnp.where(causal, s_hh, -jnp.inf)
            s_ref[hd, :half, :half] = s_ll
            s_ref[hd, :half, half:] = s_lo[:, half:]
            s_ref[hd, half:, :half] = jnp.full((half, half), -jnp.inf, F32)
            s_ref[hd, half:, half:] = s_hh
            mx_ref[hd, :, :half] = jnp.max(s_ll, axis=0, keepdims=True)
            mx_ref[hd, :, half:] = jnp.maximum(jnp.max(s_lo[:, half:], axis=0, keepdims=True),
                                               jnp.max(s_hh, axis=0, keepdims=True))

    def stage_e(i, j):
        for hd in range(2):
            m_prev = m_ref[hd]
            if fox:
                delta = goff_ref[g_base + hd, i] - goff_ref[g_base + hd, j]
                m_new = jnp.maximum(m_prev, mx_ref[hd] + delta)
                shift = m_new - delta
            else:
                m_new = jnp.maximum(m_prev, mx_ref[hd])
                shift = m_new
            p_ref[hd] = jnp.exp2(s_ref[hd] - shift).astype(BF16)
            a_ref[hd] = jnp.exp2(m_prev - m_new)
            m_ref[hd] = m_new

    def stage_p(j, diag=False):
        for hd in range(2):
            if not diag:
                acc_ref[hd] = a_ref[hd] * acc_ref[hd] + jnp.dot(
                    vt_ref[0, hd, j], p_ref[hd], preferred_element_type=F32)[:ACC_ROWS]
                continue
            acc_ref[hd, :, :half] = a_ref[hd, :, :half] * acc_ref[hd, :, :half] + jnp.dot(
                vt_ref[0, hd, j, :, :half], p_ref[hd, :half, :half],
                preferred_element_type=F32)[:ACC_ROWS]
            acc_ref[hd, :, half:] = a_ref[hd, :, half:] * acc_ref[hd, :, half:] + jnp.dot(
                vt_ref[0, hd, j], p_ref[hd, :, half:], preferred_element_type=F32)[:ACC_ROWS]

    def finish(i):
        outs = []
        for hd in range(2):
            acc = acc_ref[hd]
            outs.append(acc[0:HEAD_V, :] / acc[HEAD_V:HEAD_V + 1, :])
        o_ref[0, pl.ds(pl.multiple_of(i * t, t), t), :] = jnp.transpose(
            jnp.concatenate(outs, axis=0)).astype(o_ref.dtype)
        reset()

    def reset():
        m_ref[...] = jnp.full(m_ref.shape, -jnp.inf, F32)
        acc_ref[...] = jnp.zeros_like(acc_ref)

    def step(i, j, nxt, next_diag=False, row_start=False):
        if row_start:
            stage_p(i - 1, diag=True)
            finish(i - 1)
        else:
            stage_p(j - 1)
        stage_e(i, j)
        if nxt is not None:
            stage_q(nxt[0], nxt[1], next_diag)

    def plain_steps(i, j0, count):
        for u in range(count):
            step(i, j0 + u, (i, j0 + u + 1))

    def plain_run(i, j0, n_plain):
        def pair(_, j):
            plain_steps(i, j, 2)
            return j + 2

        j = lax.fori_loop(0, lax.shift_right_logical(n_plain, 1), pair, j0)

        @pl.when(lax.bitwise_and(n_plain, 1) != 0)
        def _():
            plain_steps(i, j, 1)

    def first_tile(i):
        if not fox:
            return jnp.int32(0)

        def negligible(j):
            ok = None
            for hd in range(2):
                r = g_base + hd
                bound = (goff_ref[r, i] - goff_ref[r, j + 1]
                         + qn_ref[r, i] * (kn_ref[r, j] + kn_ref[r, i]))
                small = bound < -SKIP_BITS
                ok = small if ok is None else jnp.logical_and(ok, small)
            return ok

        return lax.while_loop(lambda j: jnp.logical_and(j < i - 2, negligible(j)),
                              lambda j: j + 1, jnp.int32(0))

    last = nblk - 1
    reset()
    stage_q(0, 0, True)
    stage_e(0, 0)
    stage_q(1, 0, False)
    step(1, 0, (1, 1), next_diag=True, row_start=True)
    step(1, 1, (2, 0))
    step(2, 0, (2, 1), row_start=True)

    def rest_of_row(i, first):
        plain_run(i, first + 1, i - 2 - first)
        step(i, i - 1, (i, i), next_diag=True)

    def row(i, first):
        rest_of_row(i, first)
        nxt_first = first_tile(i + 1)
        step(i, i, (i + 1, nxt_first))
        step(i + 1, nxt_first, (i + 1, nxt_first + 1), row_start=True)
        return nxt_first

    first = lax.fori_loop(2, last, row, jnp.int32(0))
    rest_of_row(last, first)
    step(last, last, None)
    stage_p(last, diag=True)
    finish(last)


def _attn_call(q, k, vt, tables=None):
    B, S, _ = q.shape
    fox = tables is not None
    t = BLK
    nblk = S // t
    assert nblk >= 3, "the step schedule peels q blocks 0..2"
    in_specs = [
        pl.BlockSpec((1, S, 2 * HEAD_PAD), lambda b, p: (b, 0, p)),
        pl.BlockSpec((1, S, 2 * HEAD_PAD), lambda b, p: (b, 0, p)),
        pl.BlockSpec((1, 2, nblk, VT_ROWS, t), lambda b, p: (b, p, 0, 0, 0)),
    ]
    args = [q, k, vt]
    if fox:
        in_specs = [pl.BlockSpec(memory_space=pltpu.SMEM)] * len(tables) + in_specs
        args = list(tables) + args
    stat = pltpu.VMEM((2, 1, t), F32)
    return pl.pallas_call(
        functools.partial(_attn_kernel, fox=fox, nblk=nblk),
        grid=(B, N_PAIRS),
        in_specs=in_specs,
        out_specs=pl.BlockSpec((1, S, 2 * HEAD_V), lambda b, p: (b, 0, p)),
        out_shape=jax.ShapeDtypeStruct((B, S, HEADS * HEAD_V), BF16),
        scratch_shapes=[pltpu.VMEM((2, t, t), F32), pltpu.VMEM((2, t, t), BF16),
                        stat, stat, stat, pltpu.VMEM((2, ACC_ROWS, t), F32)],
        compiler_params=pltpu.CompilerParams(
            dimension_semantics=("parallel", "parallel"),
            vmem_limit_bytes=VMEM_LIMIT),
        name="attn_fox" if fox else "attn_mla",
    )(*args)


def _post_kernel(om_ref, of_ref, ga_ref, gb_ref, x_ref, wom_ref, wof_ref, wout_ref,
                 g1_ref, g2_ref, w1_ref, w2_ref, g3_ref, out_ref):
    y_mla = jnp.dot(om_ref[0], wom_ref[...], preferred_element_type=F32)
    y_fox = jnp.dot(of_ref[0], wof_ref[...], preferred_element_type=F32)
    merged = ga_ref[0].astype(F32) * y_mla + gb_ref[0].astype(F32) * y_fox
    mix = jnp.dot(merged.astype(BF16), wout_ref[...], preferred_element_type=F32)
    x1 = x_ref[0] + _rms(mix, g1_ref[...])
    h2 = _rms(x1, g2_ref[...]).astype(BF16)
    m = jnp.zeros_like(x1)
    for c in range(D_FF // FF_CHUNK):
        sl = slice(c * FF_CHUNK, (c + 1) * FF_CHUNK)
        u = jnp.dot(h2, w1_ref[:, sl], preferred_element_type=F32)
        a = jnp.square(jnp.maximum(u, 0.0)).astype(BF16)
        m = m + jnp.dot(a, w2_ref[sl, :], preferred_element_type=F32)
    out_ref[0] = x1 + _rms(m, g3_ref[...])


def _post_call(o_mla, o_fox, ga, gb, x, wom, wof, wout, g1, g2, w1, w2, g3):
    B, S, D = x.shape
    tm = POST_TM
    tok = lambda n: pl.BlockSpec((1, tm, n), lambda b, j: (b, j, 0))
    in_specs = [tok(512), tok(512), tok(D), tok(D), tok(D)] + [
        _const_spec(a.shape) for a in (wom, wof, wout, g1, g2, w1, w2, g3)]
    return pl.pallas_call(
        _post_kernel,
        grid=(B, S // tm),
        in_specs=in_specs,
        out_specs=tok(D),
        out_shape=jax.ShapeDtypeStruct((B, S, D), F32),
        compiler_params=pltpu.CompilerParams(
            dimension_semantics=("parallel", "parallel"), vmem_limit_bytes=VMEM_LIMIT),
        name="post",
    )(o_mla, o_fox, ga, gb, x, wom, wof, wout, g1, g2, w1, w2, g3)


def _pad_heads(w, width):
    r = w.shape[0]
    w = w.reshape(r, HEADS, width)
    w = jnp.pad(w, ((0, 0), (0, 0), (0, HEAD_PAD - width)))
    return w.reshape(r, HEADS * HEAD_PAD)


def _arrange_in_proj(w, b):
    offs = np.cumsum((0,) + IN_SPLITS)
    cq, ckv, kr, fq, fk, fv, f, ga, gb = [w[:, offs[i]:offs[i + 1]] for i in range(9)]
    bcq, bckv, bkr, bfq, bfk, bfv, bf, bga, bgb = [b[None, offs[i]:offs[i + 1]] for i in range(9)]
    kr_pad = ((0, 0), (MLA_NOPE_DIM, LANES - MLA_NOPE_DIM - MLA_ROPE_DIM))
    f_rep = lambda a: jnp.pad(jnp.repeat(a, N_PIECES, axis=1), ((0, 0), (0, LANES - PIECE_LANES)))
    w_r = jnp.concatenate([cq, ckv, jnp.pad(kr, kr_pad), fq, fk, f_rep(f), 0.5 * ga, 0.5 * gb],
                          axis=1)
    b_r = jnp.concatenate([bcq, bckv, jnp.pad(bkr, kr_pad), bfq, bfk, f_rep(bf),
                           0.5 * bga, 0.5 * bgb], axis=1)
    return w_r.astype(BF16), b_r.astype(F32), fv.T.astype(BF16), bfv.T.astype(F32)


def _static_tables():
    inv = ROPE_THETA ** (-np.arange(HALF_ROPE, dtype=np.float32) / HALF_ROPE)
    freq = inv.reshape(HALF_ROPE, 1)
    piece = np.full((1, LANES), float(N_PIECES), np.float32)
    piece[0, :PIECE_LANES] = np.arange(PIECE_LANES) % N_PIECES
    vfill = np.zeros((VT_ROWS - HEAD_V, BLK), np.float32)
    vfill[0, :] = 1.0
    ind = np.zeros((FOX_HEADS * FOX_HEAD_DIM, LANES), np.float32)
    for hd in range(FOX_HEADS):
        ind[hd * FOX_HEAD_DIM:(hd + 1) * FOX_HEAD_DIM, hd] = 1.0
    return (jnp.asarray(freq), jnp.asarray(piece), jnp.asarray(vfill, BF16),
            jnp.asarray(ind, BF16))


def _layer(x, pos_row, ln_pre_mix, ln_post_mix, ln_pre_mlp, ln_post_mlp, w_in, b_in, q_a_norm,
           w_uq, kv_a_norm, w_uk, w_uv, w_o_mla, w_o_fox, w_out, w_ff1, w_ff2):
    B, S, _ = x.shape
    row = lambda v: v[None, :].astype(F32)
    w_r, b_r, wfvt, bfvt = _arrange_in_proj(w_in, b_in)
    freq, piece, vfill, ind = _static_tables()
    consts = (row(ln_pre_mix), w_r, b_r, row(q_a_norm),
              _pad_heads(w_uq, MLA_NOPE_DIM + MLA_ROPE_DIM).astype(BF16), row(kv_a_norm),
              _pad_heads(w_uk, MLA_NOPE_DIM).astype(BF16), w_uv.T.astype(BF16), wfvt, bfvt,
              freq, piece, vfill, ind)
    (q_mla, k_mla, vt_mla, q_fox, k_fox, vt_fox, goff, norms, ga, gb) = _pre_call(
        x, pos_row, consts)

    per_head = lambda a: jnp.transpose(a, (0, 2, 1)).reshape(B * FOX_HEADS, S // BLK)
    tables = (per_head(goff[:, :, 0, 0:PIECE_LANES:N_PIECES]),
              per_head(norms[:, :, 0, :FOX_HEADS]), per_head(norms[:, :, 1, :FOX_HEADS]))

    o_mla = _attn_call(q_mla, k_mla, vt_mla)
    o_fox = _attn_call(q_fox, k_fox, vt_fox, tables)

    return _post_call(o_mla, o_fox, ga, gb, x, w_o_mla.astype(BF16), w_o_fox.astype(BF16),
                      w_out.astype(BF16), row(ln_post_mix), row(ln_pre_mlp),
                      w_ff1.astype(BF16), w_ff2.astype(BF16), row(ln_post_mlp))


def kernel(x, positions, ln_pre_mix, ln_post_mix, ln_pre_mlp, ln_post_mlp, w_in, b_in,
           q_a_norm, w_uq, kv_a_norm, w_uk, w_uv, w_o_mla, w_o_fox, w_out, w_ff1, w_ff2):
    pos_row = positions[:, None, :]
    for l in range(w_in.shape[0]):
        x = _layer(x, pos_row, ln_pre_mix[l], ln_post_mix[l], ln_pre_mlp[l], ln_post_mlp[l],
                   w_in[l], b_in[l], q_a_norm[l], w_uq[l], kv_a_norm[l], w_uk[l], w_uv[l],
                   w_o_mla[l], w_o_fox[l], w_out[l], w_ff1[l], w_ff2[l])
    return x
```

```python
import functools

import numpy as np
import jax
import jax.numpy as jnp
from jax import lax
from jax.experimental import pallas as pl
from jax.experimental.pallas import tpu as pltpu

F32 = jnp.float32
BF16 = jnp.bfloat16

D_MODEL = 1024
MLA_HEADS = 8
MLA_Q_LORA = 256
MLA_KV_LORA = 128
MLA_NOPE_DIM = 64
MLA_ROPE_DIM = 32
MLA_V_DIM = 64
FOX_HEADS = 8
FOX_HEAD_DIM = 64
D_FF = 4 * D_MODEL
ROPE_THETA = 10000.0
NORM_EPS = 1e-6
IN_SPLITS = (MLA_Q_LORA, MLA_KV_LORA, MLA_ROPE_DIM,
             FOX_HEADS * FOX_HEAD_DIM, FOX_HEADS * FOX_HEAD_DIM, FOX_HEADS * FOX_HEAD_DIM,
             FOX_HEADS, D_MODEL, D_MODEL)

LANES = 128
SUBLANES = 8
HEAD_PAD = 128
HEADS = 8
HEAD_V = 64
VT_ROWS = HEAD_V + 64
ACC_ROWS = HEAD_V + 8
HALF_ROPE = MLA_ROPE_DIM // 2
N_PAIRS = HEADS // 2
AUG = 64
N_PIECES = 3
PIECE_LANES = N_PIECES * FOX_HEADS
LOG2E = float(np.log2(np.e))
SKIP_BITS = 160.0
NORM_MARGIN = 1.02
VMEM_LIMIT = 52 * 1024 * 1024

BLK = 512
POST_TM = 512
FF_CHUNK = 512

_W = (MLA_Q_LORA, MLA_KV_LORA, LANES, FOX_HEADS * FOX_HEAD_DIM, FOX_HEADS * FOX_HEAD_DIM, LANES,
      D_MODEL, D_MODEL)
_OFF = tuple(int(v) for v in np.cumsum((0,) + _W))
(C_CQ, C_CKV, C_KR, C_FQ, C_FK, C_F, C_GA, C_GB) = tuple(
    (_OFF[i], _OFF[i + 1]) for i in range(8))


def _rms(x, g):
    return x * lax.rsqrt(jnp.mean(x * x, axis=-1, keepdims=True) + NORM_EPS) * g


def _sigmoid_of_twice(half_x):
    return 0.5 * jnp.tanh(half_x) + 0.5


def _const_spec(shape):
    zeros = (0,) * len(shape)
    return pl.BlockSpec(shape, lambda *_: zeros, pipeline_mode=pl.Buffered(1))


def _dot_nt(a, b):
    return lax.dot_general(a, b, (((1,), (1,)), ((), ())), preferred_element_type=F32)


def _pre_kernel(x_ref, pos_ref, g_ref, w_ref, b_ref, qn_ref, wuq_ref, kvn_ref, wuk_ref,
                wuvt_ref, wfvt_ref, bfvt_ref, freq_ref, piece_ref, vfill_ref, ind_ref,
                qm_ref, km_ref, vmt_ref, qf_ref, kf_ref, vft_ref, goff_ref, norm_ref,
                ga_ref, gb_ref,
                carry_ref):
    tm = x_ref.shape[1]
    lane = lax.broadcasted_iota(jnp.int32, (tm, LANES), 1)

    @pl.when(pl.program_id(1) == 0)
    def _():
        carry_ref[...] = jnp.zeros_like(carry_ref)

    h = _rms(x_ref[0], g_ref[...]).astype(BF16)

    z_all = jnp.dot(h, w_ref[...], preferred_element_type=F32) + b_ref[...]

    def proj(c):
        return z_all[:, c[0]:c[1]]

    ang = freq_ref[...] * pos_ref[0].astype(F32)
    cos_t = jnp.cos(ang)
    sin_t = jnp.sin(ang)
    below = (MLA_NOPE_DIM, tm)
    above = (LANES - MLA_NOPE_DIM - MLA_ROPE_DIM, tm)
    cos = jnp.transpose(jnp.concatenate(
        [jnp.ones(below, F32), cos_t, cos_t, jnp.ones(above, F32)], axis=0))
    sin = jnp.transpose(jnp.concatenate(
        [jnp.zeros(below, F32), -sin_t, sin_t, jnp.zeros(above, F32)], axis=0))
    s_lo = jnp.where(lane < MLA_NOPE_DIM + HALF_ROPE, sin, 0.0)
    s_hi = jnp.where(lane >= MLA_NOPE_DIM + HALF_ROPE, sin, 0.0)

    def rope(blk, tables):
        c, lo, hi = tables
        return (blk * c + pltpu.roll(blk, LANES - HALF_ROPE, 1) * lo
                + pltpu.roll(blk, HALF_ROPE, 1) * hi)

    c_q = _rms(proj(C_CQ), qn_ref[...]).astype(BF16)
    q_full = jnp.dot(c_q, wuq_ref[...], preferred_element_type=F32)
    q_scale = (MLA_NOPE_DIM + MLA_ROPE_DIM) ** -0.5 * LOG2E
    q_tables = (cos * q_scale, s_lo * q_scale, s_hi * q_scale)
    c_kv = _rms(proj(C_CKV), kvn_ref[...]).astype(BF16)
    k_nope = jnp.dot(c_kv, wuk_ref[...], preferred_element_type=F32)
    k_rot = rope(proj(C_KR), (cos, s_lo, s_hi))
    for hd in range(HEADS):
        sl = slice(hd * HEAD_PAD, (hd + 1) * HEAD_PAD)
        qm_ref[0, :, sl] = rope(q_full[:, sl], q_tables).astype(BF16)
        km_ref[0, :, sl] = (k_nope[:, sl] + k_rot).astype(BF16)
    vmt = _dot_nt(wuvt_ref[...], c_kv).astype(BF16)

    piece_id = piece_ref[...]

    def pieces_by_lane(v):
        p0 = v.astype(BF16)
        r1 = v - p0.astype(F32)
        p1 = r1.astype(BF16)
        p2 = (r1 - p1.astype(F32)).astype(BF16)
        return jnp.where(piece_id == 0.0, p0, jnp.where(piece_id == 1.0, p1, p2))

    f_logit = proj(C_F)
    log_f = jnp.minimum(f_logit, 0.0) - jnp.log1p(jnp.exp(-jnp.abs(f_logit)))
    row = lax.broadcasted_iota(jnp.int32, (tm, tm), 0)
    col = lax.broadcasted_iota(jnp.int32, (tm, tm), 1)
    tri = (row >= col).astype(BF16)
    cum = jnp.dot(tri, pieces_by_lane(log_f), preferred_element_type=F32)
    cum = cum + pltpu.roll(cum, LANES - 1, 1) + pltpu.roll(cum, LANES - 2, 1)
    local = jnp.where(piece_id == 0.0, cum, 0.0)
    goff_ref[0, 0] = carry_ref[...] * LOG2E
    carry_ref[...] = carry_ref[...] + local[tm - 1:tm, :]

    local3 = local + pltpu.roll(local, 1, 1) + pltpu.roll(local, 2, 1)
    aug = pieces_by_lane(local3 * LOG2E).astype(F32)

    fq = proj(C_FQ) * (FOX_HEAD_DIM ** -0.5 * LOG2E)
    fk = proj(C_FK)

    def max_row_norm(v):
        sq = jnp.dot((v * v).astype(BF16), ind_ref[...], preferred_element_type=F32)
        return jnp.sqrt(jnp.max(sq, axis=0, keepdims=True)) * NORM_MARGIN

    norm_ref[0, 0] = jnp.concatenate(
        [max_row_norm(fq), max_row_norm(fk), jnp.zeros((SUBLANES - 2, LANES), F32)], axis=0)
    feat = lane < FOX_HEAD_DIM
    aug_a = jnp.logical_and(lane >= AUG, lane < AUG + N_PIECES)
    aug_b = jnp.logical_and(lane >= AUG + N_PIECES, lane < AUG + 2 * N_PIECES)
    ones_a = jnp.where(aug_a, 1.0, 0.0)
    ones_b = jnp.where(aug_b, 1.0, 0.0)
    for hd in range(HEADS):
        src = slice((hd // 2) * LANES, (hd // 2 + 1) * LANES)
        dst = slice(hd * HEAD_PAD, (hd + 1) * HEAD_PAD)
        q_blk, k_blk = fq[:, src], fk[:, src]
        if hd % 2:
            q_blk = pltpu.roll(q_blk, FOX_HEAD_DIM, 1)
            k_blk = pltpu.roll(k_blk, FOX_HEAD_DIM, 1)
        a_t = pltpu.roll(aug, AUG - N_PIECES * hd, 1)
        b_s = pltpu.roll(-aug, AUG + N_PIECES - N_PIECES * hd, 1)
        qf_ref[0, :, dst] = jnp.where(feat, q_blk, jnp.where(aug_a, a_t, ones_b)).astype(BF16)
        kf_ref[0, :, dst] = jnp.where(feat, k_blk, jnp.where(aug_b, b_s, ones_a)).astype(BF16)
    vft = (_dot_nt(wfvt_ref[...], h) + bfvt_ref[...]).astype(BF16)

    vfill = vfill_ref[...]
    for hd in range(HEADS):
        rows = slice(hd * HEAD_V, (hd + 1) * HEAD_V)
        vmt_ref[0, hd, 0, 0:HEAD_V, :] = vmt[rows, :]
        vmt_ref[0, hd, 0, HEAD_V:, :] = vfill
        vft_ref[0, hd, 0, 0:HEAD_V, :] = vft[rows, :]
        vft_ref[0, hd, 0, HEAD_V:, :] = vfill

    ga_ref[0] = _sigmoid_of_twice(proj(C_GA)).astype(BF16)
    gb_ref[0] = _sigmoid_of_twice(proj(C_GB)).astype(BF16)


def _pre_call(x, pos_row, consts):
    B, S, D = x.shape
    tm = BLK
    nblk = S // tm
    tok = lambda n: pl.BlockSpec((1, tm, n), lambda b, j: (b, j, 0))
    vt_shape = jax.ShapeDtypeStruct((B, HEADS, nblk, VT_ROWS, tm), BF16)
    vt_spec = pl.BlockSpec((1, HEADS, 1, VT_ROWS, tm), lambda b, j: (b, 0, j, 0, 0))
    out_shape = [
        jax.ShapeDtypeStruct((B, S, HEADS * HEAD_PAD), BF16),
        jax.ShapeDtypeStruct((B, S, HEADS * HEAD_PAD), BF16),
        vt_shape,
        jax.ShapeDtypeStruct((B, S, HEADS * HEAD_PAD), BF16),
        jax.ShapeDtypeStruct((B, S, HEADS * HEAD_PAD), BF16),
        vt_shape,
        jax.ShapeDtypeStruct((B, nblk, 1, LANES), F32),
        jax.ShapeDtypeStruct((B, nblk, SUBLANES, LANES), F32),
        jax.ShapeDtypeStruct((B, S, D), BF16),
        jax.ShapeDtypeStruct((B, S, D), BF16),
    ]
    out_specs = [tok(HEADS * HEAD_PAD), tok(HEADS * HEAD_PAD), vt_spec,
                 tok(HEADS * HEAD_PAD), tok(HEADS * HEAD_PAD), vt_spec,
                 pl.BlockSpec((1, 1, 1, LANES), lambda b, j: (b, j, 0, 0)),
                 pl.BlockSpec((1, 1, SUBLANES, LANES), lambda b, j: (b, j, 0, 0)),
                 tok(D), tok(D)]
    in_specs = ([tok(D), pl.BlockSpec((1, 1, tm), lambda b, j: (b, 0, j))]
                + [_const_spec(a.shape) for a in consts])
    return pl.pallas_call(
        _pre_kernel,
        grid=(B, nblk),
        in_specs=in_specs,
        out_specs=out_specs,
        out_shape=out_shape,
        scratch_shapes=[pltpu.VMEM((1, LANES), F32)],
        compiler_params=pltpu.CompilerParams(
            dimension_semantics=("parallel", "arbitrary"), vmem_limit_bytes=VMEM_LIMIT),
        name="pre",
    )(x, pos_row, *consts)


def _attn_kernel(*refs, fox, nblk):
    if fox:
        goff_ref, qn_ref, kn_ref = refs[:3]
        refs = refs[3:]
    q_ref, k_ref, vt_ref, o_ref, s_ref, p_ref, mx_ref, m_ref, a_ref, acc_ref = refs
    t = BLK
    half = t // 2
    g_base = (pl.program_id(0) * N_PAIRS + pl.program_id(1)) * 2

    def stage_q(i, j, diag):
        qoff = pl.multiple_of(i * t, t)
        koff = pl.multiple_of(j * t, t)
        for hd in range(2):
            lanes = slice(hd * HEAD_PAD, (hd + 1) * HEAD_PAD)
            if not diag:
                s = _dot_nt(k_ref[0, pl.ds(koff, t), lanes], q_ref[0, pl.ds(qoff, t), lanes])
                s_ref[hd] = s
                mx_ref[hd] = jnp.max(s, axis=0, keepdims=True)
                continue
            causal = (lax.broadcasted_iota(jnp.int32, (half, half), 0)
                      <= lax.broadcasted_iota(jnp.int32, (half, half), 1))
            k_lo = k_ref[0, pl.ds(koff, half), lanes]
            k_hi = k_ref[0, pl.ds(pl.multiple_of(koff + half, half), half), lanes]
            s_lo = _dot_nt(k_lo, q_ref[0, pl.ds(qoff, t), lanes])
            s_hh = _dot_nt(k_hi, q_ref[0, pl.ds(pl.multiple_of(qoff + half, half), half), lanes])
            s_ll = jnp.where(causal, s_lo[:, :half], -jnp.inf)
            s_hh = jnp.where(causal, s_hh, -jnp.inf)
            s_ref[hd, :half, :half] = s_ll
            s_ref[hd, :half, half:] = s_lo[:, half:]
            s_ref[hd, half:, :half] = jnp.full((half, half), -jnp.inf, F32)
            s_ref[hd, half:, half:] = s_hh
            mx_ref[hd, :, :half] = jnp.max(s_ll, axis=0, keepdims=True)
            mx_ref[hd, :, half:] = jnp.maximum(jnp.max(s_lo[:, half:], axis=0, keepdims=True),
                                               jnp.max(s_hh, axis=0, keepdims=True))

    def stage_e(i, j):
        for hd in range(2):
            m_prev = m_ref[hd]
            if fox:
                delta = goff_ref[g_base + hd, i] - goff_ref[g_base + hd, j]
                m_new = jnp.maximum(m_prev, mx_ref[hd] + delta)
                shift = m_new - delta
            else:
                m_new = jnp.maximum(m_prev, mx_ref[hd])
                shift = m_new
            p_ref[hd] = jnp.exp2(s_ref[hd] - shift).astype(BF16)
            a_ref[hd] = jnp.exp2(m_prev - m_new)
            m_ref[hd] = m_new

    def stage_p(j, diag=False):
        for hd in range(2):
            if not diag:
                acc_ref[hd] = a_ref[hd] * acc_ref[hd] + jnp.dot(
                    vt_ref[0, hd, j], p_ref[hd], preferred_element_type=F32)[:ACC_ROWS]
                continue
            acc_ref[hd, :, :half] = a_ref[hd, :, :half] * acc_ref[hd, :, :half] + jnp.dot(
                vt_ref[0, hd, j, :, :half], p_ref[hd, :half, :half],
                preferred_element_type=F32)[:ACC_ROWS]
            acc_ref[hd, :, half:] = a_ref[hd, :, half:] * acc_ref[hd, :, half:] + jnp.dot(
                vt_ref[0, hd, j], p_ref[hd, :, half:], preferred_element_type=F32)[:ACC_ROWS]

    def finish(i):
        outs = []
        for hd in range(2):
            acc = acc_ref[hd]
            outs.append(acc[0:HEAD_V, :] / acc[HEAD_V:HEAD_V + 1, :])
        o_ref[0, pl.ds(pl.multiple_of(i * t, t), t), :] = jnp.transpose(
            jnp.concatenate(outs, axis=0)).astype(o_ref.dtype)
        reset()

    def reset():
        m_ref[...] = jnp.full(m_ref.shape, -jnp.inf, F32)
        acc_ref[...] = jnp.zeros_like(acc_ref)

    def step(i, j, nxt, next_diag=False, row_start=False):
        if row_start:
            stage_p(i - 1, diag=True)
            finish(i - 1)
        else:
            stage_p(j - 1)
        stage_e(i, j)
        if nxt is not None:
            stage_q(nxt[0], nxt[1], next_diag)

    def plain_steps(i, j0, count):
        for u in range(count):
            step(i, j0 + u, (i, j0 + u + 1))

    def plain_run(i, j0, n_plain):
        def pair(_, j):
            plain_steps(i, j, 2)
            return j + 2

        j = lax.fori_loop(0, lax.shift_right_logical(n_plain, 1), pair, j0)

        @pl.when(lax.bitwise_and(n_plain, 1) != 0)
        def _():
            plain_steps(i, j, 1)

    def first_tile(i):
        if not fox:
            return jnp.int32(0)

        def negligible(j):
            ok = None
            for hd in range(2):
                r = g_base + hd
                bound = (goff_ref[r, i] - goff_ref[r, j + 1]
                         + qn_ref[r, i] * (kn_ref[r, j] + kn_ref[r, i]))
                small = bound < -SKIP_BITS
                ok = small if ok is None else jnp.logical_and(ok, small)
            return ok

        return lax.while_loop(lambda j: jnp.logical_and(j < i - 2, negligible(j)),
                              lambda j: j + 1, jnp.int32(0))

    last = nblk - 1
    reset()
    stage_q(0, 0, True)
    stage_e(0, 0)
    stage_q(1, 0, False)
    step(1, 0, (1, 1), next_diag=True, row_start=True)
    step(1, 1, (2, 0))
    step(2, 0, (2, 1), row_start=True)

    def rest_of_row(i, first):
        plain_run(i, first + 1, i - 2 - first)
        step(i, i - 1, (i, i), next_diag=True)

    def row(i, first):
        rest_of_row(i, first)
        nxt_first = first_tile(i + 1)
        step(i, i, (i + 1, nxt_first))
        step(i + 1, nxt_first, (i + 1, nxt_first + 1), row_start=True)
        return nxt_first

    first = lax.fori_loop(2, last, row, jnp.int32(0))
    rest_of_row(last, first)
    step(last, last, None)
    stage_p(last, diag=True)
    finish(last)


def _attn_call(q, k, vt, tables=None):
    B, S, _ = q.shape
    fox = tables is not None
    t = BLK
    nblk = S // t
    assert nblk >= 3, "the step schedule peels q blocks 0..2"
    in_specs = [
        pl.BlockSpec((1, S, 2 * HEAD_PAD), lambda b, p: (b, 0, p)),
        pl.BlockSpec((1, S, 2 * HEAD_PAD), lambda b, p: (b, 0, p)),
        pl.BlockSpec((1, 2, nblk, VT_ROWS, t), lambda b, p: (b, p, 0, 0, 0)),
    ]
    args = [q, k, vt]
    if fox:
        in_specs = [pl.BlockSpec(memory_space=pltpu.SMEM)] * len(tables) + in_specs
        args = list(tables) + args
    stat = pltpu.VMEM((2, 1, t), F32)
    return pl.pallas_call(
        functools.partial(_attn_kernel, fox=fox, nblk=nblk),
        grid=(B, N_PAIRS),
        in_specs=in_specs,
        out_specs=pl.BlockSpec((1, S, 2 * HEAD_V), lambda b, p: (b, 0, p)),
        out_shape=jax.ShapeDtypeStruct((B, S, HEADS * HEAD_V), BF16),
        scratch_shapes=[pltpu.VMEM((2, t, t), F32), pltpu.VMEM((2, t, t), BF16),
                        stat, stat, stat, pltpu.VMEM((2, ACC_ROWS, t), F32)],
        compiler_params=pltpu.CompilerParams(
            dimension_semantics=("parallel", "parallel"),
            vmem_limit_bytes=VMEM_LIMIT),
        name="attn_fox" if fox else "attn_mla",
    )(*args)


def _post_kernel(om_ref, of_ref, ga_ref, gb_ref, x_ref, wom_ref, wof_ref, wout_ref,
                 g1_ref, g2_ref, w1_ref, w2_ref, g3_ref, out_ref):
    y_mla = jnp.dot(om_ref[0], wom_ref[...], preferred_element_type=F32)
    y_fox = jnp.dot(of_ref[0], wof_ref[...], preferred_element_type=F32)
    merged = ga_ref[0].astype(F32) * y_mla + gb_ref[0].astype(F32) * y_fox
    mix = jnp.dot(merged.astype(BF16), wout_ref[...], preferred_element_type=F32)
    x1 = x_ref[0] + _rms(mix, g1_ref[...])
    h2 = _rms(x1, g2_ref[...]).astype(BF16)
    m = jnp.zeros_like(x1)
    for c in range(D_FF // FF_CHUNK):
        sl = slice(c * FF_CHUNK, (c + 1) * FF_CHUNK)
        u = jnp.dot(h2, w1_ref[:, sl], preferred_element_type=F32)
        a = jnp.square(jnp.maximum(u, 0.0)).astype(BF16)
        m = m + jnp.dot(a, w2_ref[sl, :], preferred_element_type=F32)
    out_ref[0] = x1 + _rms(m, g3_ref[...])


def _post_call(o_mla, o_fox, ga, gb, x, wom, wof, wout, g1, g2, w1, w2, g3):
    B, S, D = x.shape
    tm = POST_TM
    tok = lambda n: pl.BlockSpec((1, tm, n), lambda b, j: (b, j, 0))
    in_specs = [tok(512), tok(512), tok(D), tok(D), tok(D)] + [
        _const_spec(a.shape) for a in (wom, wof, wout, g1, g2, w1, w2, g3)]
    return pl.pallas_call(
        _post_kernel,
        grid=(B, S // tm),
        in_specs=in_specs,
        out_specs=tok(D),
        out_shape=jax.ShapeDtypeStruct((B, S, D), F32),
        compiler_params=pltpu.CompilerParams(
            dimension_semantics=("parallel", "parallel"), vmem_limit_bytes=VMEM_LIMIT),
        name="post",
    )(o_mla, o_fox, ga, gb, x, wom, wof, wout, g1, g2, w1, w2, g3)


def _pad_heads(w, width):
    r = w.shape[0]
    w = w.reshape(r, HEADS, width)
    w = jnp.pad(w, ((0, 0), (0, 0), (0, HEAD_PAD - width)))
    return w.reshape(r, HEADS * HEAD_PAD)


def _arrange_in_proj(w, b):
    offs = np.cumsum((0,) + IN_SPLITS)
    cq, ckv, kr, fq, fk, fv, f, ga, gb = [w[:, offs[i]:offs[i + 1]] for i in range(9)]
    bcq, bckv, bkr, bfq, bfk, bfv, bf, bga, bgb = [b[None, offs[i]:offs[i + 1]] for i in range(9)]
    kr_pad = ((0, 0), (MLA_NOPE_DIM, LANES - MLA_NOPE_DIM - MLA_ROPE_DIM))
    f_rep = lambda a: jnp.pad(jnp.repeat(a, N_PIECES, axis=1), ((0, 0), (0, LANES - PIECE_LANES)))
    w_r = jnp.concatenate([cq, ckv, jnp.pad(kr, kr_pad), fq, fk, f_rep(f), 0.5 * ga, 0.5 * gb],
                          axis=1)
    b_r = jnp.concatenate([bcq, bckv, jnp.pad(bkr, kr_pad), bfq, bfk, f_rep(bf),
                           0.5 * bga, 0.5 * bgb], axis=1)
    return w_r.astype(BF16), b_r.astype(F32), fv.T.astype(BF16), bfv.T.astype(F32)


def _static_tables():
    inv = ROPE_THETA ** (-np.arange(HALF_ROPE, dtype=np.float32) / HALF_ROPE)
    freq = inv.reshape(HALF_ROPE, 1)
    piece = np.full((1, LANES), float(N_PIECES), np.float32)
    piece[0, :PIECE_LANES] = np.arange(PIECE_LANES) % N_PIECES
    vfill = np.zeros((VT_ROWS - HEAD_V, BLK), np.float32)
    vfill[0, :] = 1.0
    ind = np.zeros((FOX_HEADS * FOX_HEAD_DIM, LANES), np.float32)
    for hd in range(FOX_HEADS):
        ind[hd * FOX_HEAD_DIM:(hd + 1) * FOX_HEAD_DIM, hd] = 1.0
    return (jnp.asarray(freq), jnp.asarray(piece), jnp.asarray(vfill, BF16),
            jnp.asarray(ind, BF16))


def _layer(x, pos_row, ln_pre_mix, ln_post_mix, ln_pre_mlp, ln_post_mlp, w_in, b_in, q_a_norm,
           w_uq, kv_a_norm, w_uk, w_uv, w_o_mla, w_o_fox, w_out, w_ff1, w_ff2):
    B, S, _ = x.shape
    row = lambda v: v[None, :].astype(F32)
    w_r, b_r, wfvt, bfvt = _arrange_in_proj(w_in, b_in)
    freq, piece, vfill, ind = _static_tables()
    consts = (row(ln_pre_mix), w_r, b_r, row(q_a_norm),
              _pad_heads(w_uq, MLA_NOPE_DIM + MLA_ROPE_DIM).astype(BF16), row(kv_a_norm),
              _pad_heads(w_uk, MLA_NOPE_DIM).astype(BF16), w_uv.T.astype(BF16), wfvt, bfvt,
              freq, piece, vfill, ind)
    (q_mla, k_mla, vt_mla, q_fox, k_fox, vt_fox, goff, norms, ga, gb) = _pre_call(
        x, pos_row, consts)

    per_head = lambda a: jnp.transpose(a, (0, 2, 1)).reshape(B * FOX_HEADS, S // BLK)
    tables = (per_head(goff[:, :, 0, 0:PIECE_LANES:N_PIECES]),
              per_head(norms[:, :, 0, :FOX_HEADS]), per_head(norms[:, :, 1, :FOX_HEADS]))

    o_mla = _attn_call(q_mla, k_mla, vt_mla)
    o_fox = _attn_call(q_fox, k_fox, vt_fox, tables)

    return _post_call(o_mla, o_fox, ga, gb, x, w_o_mla.astype(BF16), w_o_fox.astype(BF16),
                      w_out.astype(BF16), row(ln_post_mix), row(ln_pre_mlp),
                      w_ff1.astype(BF16), w_ff2.astype(BF16), row(ln_post_mlp))


def kernel(x, positions, ln_pre_mix, ln_post_mix, ln_pre_mlp, ln_post_mlp, w_in, b_in,
           q_a_norm, w_uq, kv_a_norm, w_uk, w_uv, w_o_mla, w_o_fox, w_out, w_ff1, w_ff2):
    pos_row = positions[:, None, :]
    for l in range(w_in.shape[0]):
        x = _layer(x, pos_row, ln_pre_mix[l], ln_post_mix[l], ln_pre_mlp[l], ln_post_mlp[l],
                   w_in[l], b_in[l], q_a_norm[l], w_uq[l], kv_a_norm[l], w_uk[l], w_uv[l],
                   w_o_mla[l], w_o_fox[l], w_out[l], w_ff1[l], w_ff2[l])
    return x
```
